```python
import math
import jax, jax.numpy as jnp
from jax import lax
import numpy as np

D_MODEL = 2048
BATCH = 1
SEQ = 8192
DEPTH = 1
DEC_BATCH = 16
DEC_SEQ = 32
PAST_LEN = 4096

CHUNK = 64
Q_BLOCK = 128
DA_HEADS = D_MODEL // 512
DA_DH = 128
DA_DV = 2 * DA_DH
DA_WIDTH = DA_HEADS * DA_DV
RET_HEADS = D_MODEL // 512
RET_DK = 256
RET_DV = 256
RET_WIDTH = RET_HEADS * RET_DV
MIX_WIDTH = DA_WIDTH + RET_WIDTH
ROPE_BASE = 10000.0
IN_SPLITS = (DA_HEADS * 2 * DA_DH, DA_HEADS * 2 * DA_DH, DA_WIDTH,
             RET_HEADS * RET_DK, RET_HEADS * RET_DK, RET_WIDTH, RET_WIDTH)
IN_COLS = 3 * DA_WIDTH + 2 * RET_HEADS * RET_DK + 2 * RET_WIDTH
N_GROUPS = 4
EXP_PER_GROUP = 8
N_EXPERTS = N_GROUPS * EXP_PER_GROUP
D_FF = D_MODEL // 8
TOP_K = 2
EPS = 1e-6
NEG_INF = -1e30

kernel_name = 'hymba_style_diffattn_retention_hiermoe_stream_step'


def rms_norm(x, g):
    xf = x.astype(jnp.float32)
    y = xf * lax.rsqrt(jnp.mean(jnp.square(xf), axis=-1, keepdims=True) + EPS)
    return (y * g.astype(jnp.float32)).astype(x.dtype)


def rotary(x, pos):
    half = x.shape[-1] // 2
    inv_freq = ROPE_BASE ** (-jnp.arange(half, dtype=jnp.float32) / half)
    ang = pos.astype(jnp.float32)[:, None] * inv_freq[None, :]
    cos = jnp.cos(ang)[None, :, None, :]
    sin = jnp.sin(ang)[None, :, None, :]
    xf = x.astype(jnp.float32)
    x1, x2 = xf[..., :half], xf[..., half:]
    return jnp.concatenate([x1 * cos - x2 * sin, x1 * sin + x2 * cos], axis=-1).astype(x.dtype)


def diff_lambda(lq1, lk1, lq2, lk2, lam_init):
    f = lambda a: a.astype(jnp.float32)
    return jnp.exp(jnp.sum(f(lq1) * f(lk1))) - jnp.exp(jnp.sum(f(lq2) * f(lk2))) + lam_init


def project(h, w_in, q_norm_g, k_norm_g, pos):
    B, S, _ = h.shape
    z = jnp.einsum('bsd,dc->bsc', h, w_in)
    points, acc = [], 0
    for s in IN_SPLITS[:-1]:
        acc += s
        points.append(acc)
    q_da, k_da, v_da, q_r, k_r, v_r, g_r = jnp.split(z, points, axis=-1)
    q_da = rms_norm(q_da.reshape(B, S, DA_HEADS, 2, DA_DH), q_norm_g)
    k_da = rms_norm(k_da.reshape(B, S, DA_HEADS, 2, DA_DH), k_norm_g)
    v_da = v_da.reshape(B, S, DA_HEADS, DA_DV)
    q_r = rotary(q_r.reshape(B, S, RET_HEADS, RET_DK), pos)
    k_r = rotary(k_r.reshape(B, S, RET_HEADS, RET_DK) * (RET_DK ** -0.5), pos)
    v_r = v_r.reshape(B, S, RET_HEADS, RET_DV)
    return q_da, k_da, v_da, q_r, k_r, v_r, g_r


def diff_attend(q, k, v, lam, mask):
    s = jnp.einsum('bqhmd,bkhmd->bhmqk', q, k).astype(jnp.float32) * (DA_DH ** -0.5)
    if mask is not None:
        s = jnp.where(mask, s, NEG_INF)
    p = jax.nn.softmax(s, axis=-1)
    w = p[:, :, 0] - lam * p[:, :, 1]
    return jnp.einsum('bhqk,bkhd->bqhd', w.astype(v.dtype), v)


def diff_attn_prompt(q, k, v, lam):
    B, S = q.shape[0], q.shape[1]
    n_blk = S // Q_BLOCK
    q_blocks = jnp.moveaxis(q.reshape(B, n_blk, Q_BLOCK, DA_HEADS, 2, DA_DH), 1, 0)
    key_chunk = jnp.arange(S) // CHUNK

    def one_block(args):
        q_blk, b = args
        q_chunk = (b * Q_BLOCK + jnp.arange(Q_BLOCK)) // CHUNK
        mask = key_chunk[None, :] <= q_chunk[:, None]
        return diff_attend(q_blk, k, v, lam, mask)

    o = lax.map(one_block, (q_blocks, jnp.arange(n_blk)))
    return jnp.moveaxis(o, 0, 1).reshape(B, S, DA_HEADS, DA_DV)


def retention_block(q, k, v, s0, log_gamma):
    L = q.shape[1]
    qf, kf, vf = q.astype(jnp.float32), k.astype(jnp.float32), v.astype(jnp.float32)
    idx = jnp.arange(L, dtype=jnp.float32)
    rel = idx[:, None] - idx[None, :]
    decay = jnp.where(rel >= 0, jnp.exp(log_gamma[:, None, None] * jnp.maximum(rel, 0.0)), 0.0)
    scores = jnp.einsum('blhd,bmhd->bhlm', qf, kf) * decay[None]
    o = jnp.einsum('bhlm,bmhe->blhe', scores, vf)
    q_decay = jnp.exp(log_gamma[:, None] * (idx + 1.0)[None, :])
    o = o + jnp.einsum('blhd,bhde->blhe', qf, s0) * q_decay.T[None, :, :, None]
    k_decay = jnp.exp(log_gamma[:, None] * (L - 1.0 - idx)[None, :])
    kw = kf * k_decay.T[None, :, :, None]
    s_new = s0 * jnp.exp(log_gamma * L)[None, :, None, None] + jnp.einsum('blhd,blhe->bhde', kw, vf)
    return o.astype(v.dtype), s_new


def retention_prompt(q, k, v, log_gamma):
    B, S = q.shape[0], q.shape[1]
    n = S // CHUNK
    to_chunks = lambda t: jnp.swapaxes(t.reshape(B, n, CHUNK, t.shape[2], t.shape[3]), 0, 1)
    s0 = jnp.zeros((B, RET_HEADS, RET_DK, RET_DV), jnp.float32)

    def step(s, inp):
        qc, kc, vc = inp
        o, s = retention_block(qc, kc, vc, s, log_gamma)
        return s, o

    s_final, o = lax.scan(step, s0, (to_chunks(q), to_chunks(k), to_chunks(v)))
    return jnp.swapaxes(o, 0, 1).reshape(B, S, RET_HEADS, RET_DV), s_final


def merge_heads(o_da, o_r, g_r, da_out_norm_g, ret_out_norm_g, w_out, lam_init):
    B, S = o_da.shape[0], o_da.shape[1]
    o_da = rms_norm(o_da, da_out_norm_g) * (1.0 - lam_init)
    o_r = rms_norm(o_r, ret_out_norm_g).reshape(B, S, RET_WIDTH) * jax.nn.silu(g_r)
    o = jnp.concatenate([o_da.reshape(B, S, DA_WIDTH), o_r], axis=-1)
    return jnp.einsum('bsc,cd->bsd', o, w_out)


def hier_moe(h, w_group, b_group, w_expert, b_expert, w_gate, w_up, w_down):
    gl = jnp.einsum('nd,dg->ng', h, w_group).astype(jnp.float32) + b_group.astype(jnp.float32)
    g_idx = jnp.argmax(gl, axis=-1)
    g_w = jnp.take_along_axis(jax.nn.softmax(gl, axis=-1), g_idx[:, None], axis=-1)
    el = jnp.einsum('nd,dge->nge', h, w_expert).astype(jnp.float32) + b_expert.astype(jnp.float32)
    el_sel = jnp.take_along_axis(el, g_idx[:, None, None], axis=1)[:, 0]
    top_v, top_i = lax.top_k(el_sel, TOP_K)
    w = jax.nn.softmax(top_v, axis=-1) * g_w
    expert_id = g_idx[:, None] * EXP_PER_GROUP + top_i
    gate = jnp.sum(jax.nn.one_hot(expert_id, N_EXPERTS, dtype=jnp.float32) * w[..., None], axis=1)
    a = jnp.einsum('nd,edf->nef', h, w_gate)
    u = jnp.einsum('nd,edf->nef', h, w_up)
    act = jax.nn.silu(a) * u * gate[:, :, None].astype(h.dtype)
    return jnp.einsum('nef,efd->nd', act, w_down)


def setup_inputs(seed: int = 0) -> dict:
    key = jax.random.key(seed)
    ks = jax.random.split(key, 24)
    f32 = jnp.float32
    nrm = lambda k, shape, scale: jax.random.normal(k, shape, f32) * scale
    gain = lambda k, shape: 1.0 + 0.02 * jax.random.normal(k, shape, f32)
    return {
        'x_prompt': nrm(ks[0], (BATCH, SEQ, D_MODEL), 1.0),
        'x_sample': nrm(ks[1], (DEC_BATCH, DEC_SEQ, D_MODEL), 1.0),
        'cache_k_diff': nrm(ks[2], (DEPTH, DEC_BATCH, PAST_LEN, DA_HEADS, 2 * DA_DH), 1.0),
        'cache_v_diff': nrm(ks[3], (DEPTH, DEC_BATCH, PAST_LEN, DA_HEADS, DA_DV), 1.0),
        'state_retention': nrm(ks[4], (DEPTH, DEC_BATCH, RET_HEADS, RET_DK, RET_DV), 0.5),
        'attn_norm_g': gain(ks[5], (DEPTH, D_MODEL)),
        'w_in': nrm(ks[6], (DEPTH, D_MODEL, IN_COLS), D_MODEL ** -0.5),
        'da_q_norm_g': gain(ks[7], (DEPTH, DA_DH)),
        'da_k_norm_g': gain(ks[8], (DEPTH, DA_DH)),
        'da_lambda_q1': nrm(ks[9], (DEPTH, DA_DH), 0.1),
        'da_lambda_k1': nrm(ks[10], (DEPTH, DA_DH), 0.1),
        'da_lambda_q2': nrm(ks[11], (DEPTH, DA_DH), 0.1),
        'da_lambda_k2': nrm(ks[12], (DEPTH, DA_DH), 0.1),
        'da_out_norm_g': gain(ks[13], (DEPTH, DA_DV)),
        'ret_out_norm_g': gain(ks[14], (DEPTH, RET_DV)),
        'w_out': nrm(ks[15], (DEPTH, MIX_WIDTH, D_MODEL), MIX_WIDTH ** -0.5),
        'ffn_norm_g': gain(ks[16], (DEPTH, D_MODEL)),
        'w_group': nrm(ks[17], (DEPTH, D_MODEL, N_GROUPS), D_MODEL ** -0.5),
        'b_group': nrm(ks[18], (DEPTH, N_GROUPS), 0.01),
        'w_expert': nrm(ks[19], (DEPTH, D_MODEL, N_GROUPS, EXP_PER_GROUP), D_MODEL ** -0.5),
        'b_expert': nrm(ks[20], (DEPTH, N_GROUPS, EXP_PER_GROUP), 0.01),
        'w_gate': nrm(ks[21], (DEPTH, N_EXPERTS, D_MODEL, D_FF), D_MODEL ** -0.5),
        'w_up': nrm(ks[22], (DEPTH, N_EXPERTS, D_MODEL, D_FF), D_MODEL ** -0.5),
        'w_down': nrm(ks[23], (DEPTH, N_EXPERTS, D_FF, D_MODEL), D_FF ** -0.5),
    }


def reference(x_prompt, x_sample, cache_k_diff, cache_v_diff, state_retention,
              attn_norm_g, w_in, da_q_norm_g, da_k_norm_g,
              da_lambda_q1, da_lambda_k1, da_lambda_q2, da_lambda_k2,
              da_out_norm_g, ret_out_norm_g, w_out, ffn_norm_g,
              w_group, b_group, w_expert, b_expert, w_gate, w_up, w_down):
    log_gamma = jnp.log(1.0 - 2.0 ** (-5.0 - jnp.arange(RET_HEADS, dtype=jnp.float32)))
    xp, xs = x_prompt, x_sample
    Bp, S = xp.shape[0], xp.shape[1]
    Bd, T = xs.shape[0], xs.shape[1]
    P = cache_k_diff.shape[2]
    pos_p = jnp.arange(S)
    pos_s = P + jnp.arange(T)
    kp_l, vp_l, sp_l, ks_l, vs_l, ss_l = [], [], [], [], [], []
    for l in range(DEPTH):
        lam_init = 0.8 - 0.6 * math.exp(-0.3 * l)
        lam = diff_lambda(da_lambda_q1[l], da_lambda_k1[l], da_lambda_q2[l], da_lambda_k2[l], lam_init)

        hp = rms_norm(xp, attn_norm_g[l])
        qd, kd, vd, qr, kr, vr, gr = project(hp, w_in[l], da_q_norm_g[l], da_k_norm_g[l], pos_p)
        od = diff_attn_prompt(qd, kd, vd, lam)
        orr, s_p = retention_prompt(qr, kr, vr, log_gamma)
        xp = xp + merge_heads(od, orr, gr, da_out_norm_g[l], ret_out_norm_g[l], w_out[l], lam_init)
        hf = rms_norm(xp, ffn_norm_g[l]).reshape(Bp * S, -1)
        xp = xp + hier_moe(hf, w_group[l], b_group[l], w_expert[l], b_expert[l],
                           w_gate[l], w_up[l], w_down[l]).reshape(xp.shape)
        kp_l.append(kd.reshape(Bp, S, DA_HEADS, 2 * DA_DH))
        vp_l.append(vd)
        sp_l.append(s_p.astype(xp.dtype))

        hs = rms_norm(xs, attn_norm_g[l])
        qd, kd, vd, qr, kr, vr, gr = project(hs, w_in[l], da_q_norm_g[l], da_k_norm_g[l], pos_s)
        k_all = jnp.concatenate([cache_k_diff[l].reshape(Bd, P, DA_HEADS, 2, DA_DH).astype(kd.dtype), kd], axis=1)
        v_all = jnp.concatenate([cache_v_diff[l].astype(vd.dtype), vd], axis=1)
        od = diff_attend(qd, k_all, v_all, lam, None)
        orr, s_s = retention_block(qr, kr, vr, state_retention[l].astype(jnp.float32), log_gamma)
        xs = xs + merge_heads(od, orr, gr, da_out_norm_g[l], ret_out_norm_g[l], w_out[l], lam_init)
        hf = rms_norm(xs, ffn_norm_g[l]).reshape(Bd * T, -1)
        xs = xs + hier_moe(hf, w_group[l], b_group[l], w_expert[l], b_expert[l],
                           w_gate[l], w_up[l], w_down[l]).reshape(xs.shape)
        ks_l.append(kd.reshape(Bd, T, DA_HEADS, 2 * DA_DH))
        vs_l.append(vd)
        ss_l.append(s_s.astype(xs.dtype))

    new_k_prompt = jnp.stack(kp_l, 0)
    new_v_prompt = jnp.stack(vp_l, 0)
    new_state_prompt = jnp.stack(sp_l, 0)
    new_k_sample = jnp.stack(ks_l, 0)
    new_v_sample = jnp.stack(vs_l, 0)
    new_state_sample = jnp.stack(ss_l, 0)
    return (xp, xs, new_k_prompt, new_v_prompt, new_state_prompt, new_k_sample, new_v_sample, new_state_sample)
```

```python
import functools
import math

import jax
import jax.numpy as jnp
from jax import lax
from jax.experimental import pallas as pl
from jax.experimental.pallas import tpu as pltpu

D_MODEL = 2048
CHUNK = 64
DA_HEADS = 4
DA_DH = 128
DA_DV = 2 * DA_DH
DA_WIDTH = DA_HEADS * DA_DV
RET_HEADS = 4
RET_DK = 256
RET_DV = 256
RET_WIDTH = RET_HEADS * RET_DV
IN_GROUP = 1024
N_GROUPS = 4
EXP_PER_GROUP = 8
N_EXPERTS = N_GROUPS * EXP_PER_GROUP
D_FF = D_MODEL // 8
EPS = 1e-6
NEG_INF = -1e30
ROPE_BASE = 10000.0
LAM_INIT = 0.8 - 0.6 * math.exp(-0.3 * 0)

LANES = 128
VMEM_LIMIT = 48 * 1024 * 1024

F32 = jnp.float32
BF16 = jnp.bfloat16


def _params(*sem):
    return pltpu.CompilerParams(dimension_semantics=sem, vmem_limit_bytes=VMEM_LIMIT)


def _sigmoid(x):
    return 1.0 / (1.0 + jnp.exp(-x))


def _rms(x):
    return x * lax.rsqrt(jnp.mean(x * x, axis=-1, keepdims=True) + EPS)


def _rmsnorm_body(x_ref, g_ref, o_ref):
    o_ref[...] = (_rms(x_ref[...]) * g_ref[...]).astype(o_ref.dtype)


def _rmsnorm(x, g, tm):
    t, d = x.shape
    return pl.pallas_call(
        _rmsnorm_body,
        out_shape=jax.ShapeDtypeStruct((t, d), BF16),
        grid=(t // tm,),
        in_specs=[pl.BlockSpec((tm, d), lambda i: (i, 0)),
                  pl.BlockSpec((1, d), lambda i: (0, 0))],
        out_specs=pl.BlockSpec((tm, d), lambda i: (i, 0)),
        compiler_params=_params("parallel"),
        name="attn_norm",
    )(x, g.reshape(1, d))


def _rope_table_body(pos0, tr, invf_ref, cos_ref, sin_ref):
    row = lax.broadcasted_iota(jnp.int32, (tr, LANES), 0) + (pl.program_id(0) * tr + pos0)
    ang = row.astype(F32) * invf_ref[...]
    cos_ref[...] = jnp.cos(ang)
    sin_ref[...] = jnp.sin(ang)


def _rope_tables(n_pos, pos0):
    half = RET_DK // 2
    inv_freq = (ROPE_BASE ** (-jnp.arange(half, dtype=F32) / half)).reshape(1, half)
    tr = min(n_pos, 512)
    spec = pl.BlockSpec((tr, half), lambda i: (i, 0))
    return pl.pallas_call(
        functools.partial(_rope_table_body, pos0, tr),
        out_shape=(jax.ShapeDtypeStruct((n_pos, half), F32),) * 2,
        grid=(n_pos // tr,),
        in_specs=[pl.BlockSpec((1, half), lambda i: (0, 0))],
        out_specs=(spec, spec),
        compiler_params=_params("parallel"),
        name="rope_tables",
    )(inv_freq)


def _store_all(z, outs):
    for o in outs:
        o[...] = z.astype(o.dtype)


def _proj_plain_body(h_ref, w_ref, *outs):
    z = jnp.dot(h_ref[...], w_ref[...], preferred_element_type=F32)
    _store_all(z, outs)


def _proj_qknorm_body(scale, h_ref, w_ref, g_ref, *outs):
    z = jnp.dot(h_ref[...], w_ref[...], preferred_element_type=F32)
    g = g_ref[...]
    for c in range(IN_GROUP // DA_DH):
        sl = slice(c * DA_DH, (c + 1) * DA_DH)
        zc = _rms(z[:, sl]) * g
        for o in outs:
            o[:, sl] = (zc * scale).astype(o.dtype) if o.dtype == BF16 else zc


def _proj_rotary_body(scale, h_ref, w_ref, cos_ref, sin_ref, o_ref):
    z = jnp.dot(h_ref[...], w_ref[...], preferred_element_type=F32) * scale
    cos = cos_ref[...]
    sin = sin_ref[...]
    half = RET_DK // 2
    for hd in range(RET_HEADS):
        x1 = z[:, hd * RET_DK: hd * RET_DK + half]
        x2 = z[:, hd * RET_DK + half: (hd + 1) * RET_DK]
        o_ref[:, hd * RET_DK: hd * RET_DK + half] = (x1 * cos - x2 * sin).astype(o_ref.dtype)
        o_ref[:, hd * RET_DK + half: (hd + 1) * RET_DK] = (x1 * sin + x2 * cos).astype(o_ref.dtype)


def _proj(h, w_in, group, body, extra, extra_specs, out_dtypes, tm, name):
    t, d = h.shape
    row = lambda i: (i, 0)
    outs = tuple(jax.ShapeDtypeStruct((t, IN_GROUP), dt) for dt in out_dtypes)
    res = pl.pallas_call(
        body,
        out_shape=outs,
        grid=(t // tm,),
        in_specs=[pl.BlockSpec((tm, d), row),
                  pl.BlockSpec((d, IN_GROUP), lambda i: (0, group))] + extra_specs,
        out_specs=tuple(pl.BlockSpec((tm, IN_GROUP), row) for _ in out_dtypes),
        compiler_params=_params("parallel"),
        name=name,
    )(h, w_in, *extra)
    return res


def _project(h, w_in, q_norm_g, k_norm_g, cos, sin, tm):
    half = RET_DK // 2
    row = lambda i: (i, 0)
    gspec = [pl.BlockSpec((1, DA_DH), lambda i: (0, 0))]
    rspec = [pl.BlockSpec((tm, half), row), pl.BlockSpec((tm, half), row)]
    (q_da,) = _proj(h, w_in, 0, functools.partial(_proj_qknorm_body, DA_DH ** -0.5),
                    [q_norm_g.reshape(1, DA_DH)], gspec, [BF16], tm, "proj_q_da")
    k_da, k_da_b = _proj(h, w_in, 1, functools.partial(_proj_qknorm_body, 1.0),
                         [k_norm_g.reshape(1, DA_DH)], gspec, [F32, BF16], tm, "proj_k_da")
    v_da, v_da_b = _proj(h, w_in, 2, _proj_plain_body, [], [], [F32, BF16], tm, "proj_v_da")
    (q_r,) = _proj(h, w_in, 3, functools.partial(_proj_rotary_body, 1.0),
                   [cos, sin], rspec, [BF16], tm, "proj_q_ret")
    (k_r,) = _proj(h, w_in, 4, functools.partial(_proj_rotary_body, RET_DK ** -0.5),
                   [cos, sin], rspec, [BF16], tm, "proj_k_ret")
    (v_r,) = _proj(h, w_in, 5, _proj_plain_body, [], [], [BF16], tm, "proj_v_ret")
    (g_r,) = _proj(h, w_in, 6, _proj_plain_body, [], [], [F32], tm, "proj_g_ret")
    return q_da, k_da, k_da_b, v_da, v_da_b, q_r, k_r, v_r, g_r


def _diff_lambda(lq1, lk1, lq2, lk2):
    s1 = jnp.sum(lq1[...] * lk1[...], axis=-1, keepdims=True)
    s2 = jnp.sum(lq2[...] * lk2[...], axis=-1, keepdims=True)
    return jnp.exp(s1) - jnp.exp(s2) + LAM_INIT


def _softmax_step(q, k, v, m_ref, l_ref, acc_ref, idx, mask):
    s = lax.dot_general(q, k, (((1,), (1,)), ((), ())), preferred_element_type=F32)
    if mask is not None:
        s = jnp.where(mask, s, NEG_INF)
    m_prev = m_ref[idx]
    m_new = jnp.maximum(m_prev, jnp.max(s, axis=-1, keepdims=True))
    alpha = jnp.exp(m_prev - m_new)
    p = jnp.exp(s - m_new)
    l_ref[idx] = alpha * l_ref[idx] + jnp.sum(p, axis=-1, keepdims=True)
    acc_ref[idx] = alpha * acc_ref[idx] + jnp.dot(p.astype(BF16), v, preferred_element_type=F32)
    m_ref[idx] = m_new


def _softmax_init(m_ref, l_ref, acc_ref):
    m_ref[...] = jnp.full(m_ref.shape, NEG_INF, F32)
    l_ref[...] = jnp.zeros(l_ref.shape, F32)
    acc_ref[...] = jnp.zeros(acc_ref.shape, F32)


def _diff_combine(lam, gain, l_ref, acc_ref, i0, i1):
    o = acc_ref[i0] / l_ref[i0] - lam * (acc_ref[i1] / l_ref[i1])
    return _rms(o) * gain * (1.0 - LAM_INIT)


def _da_prompt_body(tq, lq1, lk1, lq2, lk2, gain_ref, q_ref, k_ref, v_ref, o_ref,
                    m_ref, l_ref, acc_ref):
    qi = pl.program_id(1)
    _softmax_init(m_ref, l_ref, acc_ref)

    def block(start, mask):
        k = k_ref[pl.ds(start, tq), :]
        v = v_ref[pl.ds(start, tq), :]
        for mp in range(2):
            sl = slice(mp * DA_DH, (mp + 1) * DA_DH)
            _softmax_step(q_ref[:, sl], k[:, sl], v, m_ref, l_ref, acc_ref, mp, mask)

    def full_block(kb, carry):
        block(pl.multiple_of(kb * tq, tq), None)
        return carry

    lax.fori_loop(0, qi, full_block, 0)
    rc = lax.shift_right_logical(lax.broadcasted_iota(jnp.int32, (tq, tq), 0), 6)
    cc = lax.shift_right_logical(lax.broadcasted_iota(jnp.int32, (tq, tq), 1), 6)
    block(pl.multiple_of(qi * tq, tq), cc <= rc)

    lam = _diff_lambda(lq1, lk1, lq2, lk2)
    o_ref[...] = _diff_combine(lam, gain_ref[...], l_ref, acc_ref, 0, 1).astype(o_ref.dtype)


def _lambda_args(lq1, lk1, lq2, lk2):
    return [a.reshape(1, DA_DH) for a in (lq1, lk1, lq2, lk2)]


def _da_prompt(q, k, v, lams, out_gain, tq):
    s = q.shape[0]
    assert CHUNK == 64 and tq % CHUNK == 0 and s % tq == 0
    const = lambda h, i: (0, 0)
    vec = pl.BlockSpec((1, DA_DH), const)
    return pl.pallas_call(
        functools.partial(_da_prompt_body, tq),
        out_shape=jax.ShapeDtypeStruct((s, DA_WIDTH), BF16),
        grid=(DA_HEADS, s // tq),
        in_specs=[vec, vec, vec, vec,
                  pl.BlockSpec((1, DA_DV), const),
                  pl.BlockSpec((tq, DA_DV), lambda h, i: (i, h)),
                  pl.BlockSpec((s, DA_DV), lambda h, i: (0, h)),
                  pl.BlockSpec((s, DA_DV), lambda h, i: (0, h))],
        out_specs=pl.BlockSpec((tq, DA_DV), lambda h, i: (i, h)),
        scratch_shapes=[pltpu.VMEM((2, tq, 1), F32), pltpu.VMEM((2, tq, 1), F32),
                        pltpu.VMEM((2, tq, DA_DV), F32)],
        compiler_params=_params("parallel", "parallel"),
        name="diff_attn_prompt",
    )(*lams, out_gain.reshape(1, DA_DV), q, k, v)


def _da_sample_body(lq1, lk1, lq2, lk2, gain_ref, q_ref, kn_ref, vn_ref, ck_ref, cv_ref, o_ref,
                    m_ref, l_ref, acc_ref):
    kb = pl.program_id(1)

    def attend(k_of, v_of):
        for hd in range(DA_HEADS):
            k = k_of(hd)
            v = v_of(hd)
            for mp in range(2):
                q = q_ref[:, hd * DA_DV + mp * DA_DH: hd * DA_DV + (mp + 1) * DA_DH]
                _softmax_step(q, k[:, mp * DA_DH:(mp + 1) * DA_DH], v,
                              m_ref, l_ref, acc_ref, 2 * hd + mp, None)

    @pl.when(kb == 0)
    def _():
        _softmax_init(m_ref, l_ref, acc_ref)
        attend(lambda hd: kn_ref[:, hd * DA_DV:(hd + 1) * DA_DV],
               lambda hd: vn_ref[:, hd * DA_DV:(hd + 1) * DA_DV])

    attend(lambda hd: ck_ref[0, :, hd * DA_DV:(hd + 1) * DA_DV].astype(BF16),
           lambda hd: cv_ref[0, :, hd * DA_DV:(hd + 1) * DA_DV].astype(BF16))

    @pl.when(kb == pl.num_programs(1) - 1)
    def _():
        lam = _diff_lambda(lq1, lk1, lq2, lk2)
        gain = gain_ref[...]
        for hd in range(DA_HEADS):
            o = _diff_combine(lam, gain, l_ref, acc_ref, 2 * hd, 2 * hd + 1)
            o_ref[:, hd * DA_DV:(hd + 1) * DA_DV] = o.astype(o_ref.dtype)


def _da_sample(q, k_new, v_new, cache_k, cache_v, lams, out_gain, t, tk):
    b, p, _ = cache_k.shape
    const = lambda i, j: (0, 0)
    vec = pl.BlockSpec((1, DA_DH), const)
    tok = pl.BlockSpec((t, DA_WIDTH), lambda i, j: (i, 0))
    cache = pl.BlockSpec((1, tk, DA_WIDTH), lambda i, j: (i, j, 0))
    nmap = 2 * DA_HEADS
    return pl.pallas_call(
        _da_sample_body,
        out_shape=jax.ShapeDtypeStruct((b * t, DA_WIDTH), BF16),
        grid=(b, p // tk),
        in_specs=[vec, vec, vec, vec, pl.BlockSpec((1, DA_DV), const),
                  tok, tok, tok, cache, cache],
        out_specs=tok,
        scratch_shapes=[pltpu.VMEM((nmap, t, 1), F32), pltpu.VMEM((nmap, t, 1), F32),
                        pltpu.VMEM((nmap, t, DA_DV), F32)],
        compiler_params=_params("parallel", "arbitrary"),
        name="diff_attn_sample",
    )(*lams, out_gain.reshape(1, DA_DV), q, k_new, v_new, cache_k, cache_v)


def _retention_body(lc, lg_ref, gain_ref, q_ref, k_ref, v_ref, g_ref, s0_ref, o_ref, s_out_ref,
                    state_ref):
    c = pl.program_id(2)

    @pl.when(c == 0)
    def _():
        state_ref[...] = s0_ref[0, 0]

    lg = lg_ref[0][:, :1]
    q = q_ref[...]
    k = k_ref[...]
    v = v_ref[...]
    rel = (lax.broadcasted_iota(jnp.int32, (lc, lc), 0)
           - lax.broadcasted_iota(jnp.int32, (lc, lc), 1)).astype(F32)
    decay = jnp.where(rel >= 0, jnp.exp(lg * jnp.maximum(rel, 0.0)), 0.0)
    scores = lax.dot_general(q, k, (((1,), (1,)), ((), ())), preferred_element_type=F32) * decay
    idx = lax.broadcasted_iota(jnp.int32, (lc, 1), 0).astype(F32)
    state = state_ref[...]
    o = jnp.dot(scores.astype(BF16), v, preferred_element_type=F32)
    o = o + jnp.dot(q, state.astype(BF16), preferred_element_type=F32) * jnp.exp(lg * (idx + 1.0))
    kw = k.astype(F32) * jnp.exp(lg * (lc - 1.0 - idx))
    s_new = state * jnp.exp(lg * lc) + jnp.dot(kw.T.astype(BF16), v, preferred_element_type=F32)
    state_ref[...] = s_new

    g = g_ref[...]
    o_ref[...] = (_rms(o) * gain_ref[...] * (g * _sigmoid(g))).astype(o_ref.dtype)

    @pl.when(c == pl.num_programs(2) - 1)
    def _():
        s_out_ref[0, 0] = s_new


def _retention(q, k, v, g, s0, out_gain, seq, lc):
    b = s0.shape[0]
    nc = seq // lc
    log_gamma = jnp.log(1.0 - 2.0 ** (-5.0 - jnp.arange(RET_HEADS, dtype=F32)))
    lg = jnp.broadcast_to(log_gamma.reshape(RET_HEADS, 1, 1), (RET_HEADS, 1, LANES))
    tok = pl.BlockSpec((lc, RET_DV), lambda bi, h, c: (bi * nc + c, h))
    st = pl.BlockSpec((1, 1, RET_DK, RET_DV), lambda bi, h, c: (bi, h, 0, 0))
    return pl.pallas_call(
        functools.partial(_retention_body, lc),
        out_shape=(jax.ShapeDtypeStruct((b * seq, RET_WIDTH), BF16),
                   jax.ShapeDtypeStruct(s0.shape, F32)),
        grid=(b, RET_HEADS, nc),
        in_specs=[pl.BlockSpec((1, 1, LANES), lambda bi, h, c: (h, 0, 0)),
                  pl.BlockSpec((1, RET_DV), lambda bi, h, c: (0, 0)),
                  tok, tok, tok, tok, st],
        out_specs=(tok, st),
        scratch_shapes=[pltpu.VMEM((RET_DK, RET_DV), F32)],
        compiler_params=_params("parallel", "parallel", "arbitrary"),
        name="retention",
    )(lg, out_gain.reshape(1, RET_DV), q, k, v, g, s0)


def _route(logits):
    lane = lax.broadcasted_iota(jnp.int32, logits.shape, 1)
    big = jnp.int32(LANES)
    neg = -jnp.inf
    gl = jnp.where((lane >= N_EXPERTS) & (lane < N_EXPERTS + N_GROUPS), logits, neg)
    g_max = jnp.max(gl, axis=-1, keepdims=True)
    g_idx = jnp.min(jnp.where(gl == g_max, lane - N_EXPERTS, big), axis=-1, keepdims=True)
    g_w = 1.0 / jnp.sum(jnp.exp(gl - g_max), axis=-1, keepdims=True)
    in_group = (lane < N_EXPERTS) & (lax.shift_right_logical(lane, 3) == g_idx)
    el = jnp.where(in_group, logits, neg)
    v1 = jnp.max(el, axis=-1, keepdims=True)
    i1 = jnp.min(jnp.where(el == v1, lane, big), axis=-1, keepdims=True)
    el2 = jnp.where(lane == i1, neg, el)
    v2 = jnp.max(el2, axis=-1, keepdims=True)
    i2 = jnp.min(jnp.where(el2 == v2, lane, big), axis=-1, keepdims=True)
    e2 = jnp.exp(v2 - v1)
    w1 = g_w / (1.0 + e2)
    w2 = g_w * e2 / (1.0 + e2)
    return jnp.where(lane == i1, w1, 0.0) + jnp.where(lane == i2, w2, 0.0)


def _outproj_body(x_ref, oda_ref, ort_ref, wo_ref, gf_ref, wr_hi_ref, wr_lo_ref, br_ref,
                  x1_ref, hf_ref, gate_ref):
    x1 = (x_ref[...]
          + jnp.dot(oda_ref[...], wo_ref[:DA_WIDTH, :], preferred_element_type=F32)
          + jnp.dot(ort_ref[...], wo_ref[DA_WIDTH:, :], preferred_element_type=F32))
    x1_ref[...] = x1
    hf = _rms(x1) * gf_ref[...]
    hf_hi = hf.astype(BF16)
    hf_ref[...] = hf_hi
    hf_lo = (hf - hf_hi.astype(F32)).astype(BF16)
    wr_hi = wr_hi_ref[...]
    logits = (jnp.dot(hf_hi, wr_hi, preferred_element_type=F32)
              + jnp.dot(hf_lo, wr_hi, preferred_element_type=F32)
              + jnp.dot(hf_hi, wr_lo_ref[...], preferred_element_type=F32)
              + br_ref[...])
    gate_ref[...] = _route(logits)


def _outproj(x, o_da, o_ret, w_out, ffn_g, wr_hi, wr_lo, br, tm):
    t, d = x.shape
    row = lambda i: (i, 0)
    const = lambda i: (0, 0)
    return pl.pallas_call(
        _outproj_body,
        out_shape=(jax.ShapeDtypeStruct((t, d), F32), jax.ShapeDtypeStruct((t, d), BF16),
                   jax.ShapeDtypeStruct((t, LANES), F32)),
        grid=(t // tm,),
        in_specs=[pl.BlockSpec((tm, d), row),
                  pl.BlockSpec((tm, DA_WIDTH), row),
                  pl.BlockSpec((tm, RET_WIDTH), row),
                  pl.BlockSpec((d, d), const),
                  pl.BlockSpec((1, d), const),
                  pl.BlockSpec((d, LANES), const),
                  pl.BlockSpec((d, LANES), const),
                  pl.BlockSpec((1, LANES), const)],
        out_specs=(pl.BlockSpec((tm, d), row), pl.BlockSpec((tm, d), row),
                   pl.BlockSpec((tm, LANES), row)),
        compiler_params=_params("parallel"),
        name="out_proj_router",
    )(x, o_da, o_ret, w_out, ffn_g.reshape(1, d), wr_hi, wr_lo, br)


def _moe_body(hf_ref, gate_ref, x1_ref, wg_ref, wu_ref, wd_ref, y_ref):
    e = pl.program_id(1)

    @pl.when(e == 0)
    def _():
        y_ref[...] = x1_ref[...]

    h = hf_ref[...]
    a = jnp.dot(h, wg_ref[0], preferred_element_type=F32)
    u = jnp.dot(h, wu_ref[0], preferred_element_type=F32)
    gate = gate_ref[...]
    lane = lax.broadcasted_iota(jnp.int32, gate.shape, 1)
    ge = jnp.sum(jnp.where(lane == e, gate, 0.0), axis=-1, keepdims=True)
    act = (a * _sigmoid(a)) * u * ge
    y_ref[...] += jnp.dot(act.astype(BF16), wd_ref[0], preferred_element_type=F32)


def _moe(hf, gate, x1, w_gate, w_up, w_down, tm):
    t, d = hf.shape
    row = lambda i, e: (i, 0)
    return pl.pallas_call(
        _moe_body,
        out_shape=jax.ShapeDtypeStruct((t, d), F32),
        grid=(t // tm, N_EXPERTS),
        in_specs=[pl.BlockSpec((tm, d), row),
                  pl.BlockSpec((tm, LANES), row),
                  pl.BlockSpec((tm, d), row),
                  pl.BlockSpec((1, d, D_FF), lambda i, e: (e, 0, 0)),
                  pl.BlockSpec((1, d, D_FF), lambda i, e: (e, 0, 0)),
                  pl.BlockSpec((1, D_FF, d), lambda i, e: (e, 0, 0))],
        out_specs=pl.BlockSpec((tm, d), row),
        compiler_params=_params("parallel", "arbitrary"),
        name="moe",
    )(hf, gate, x1, w_gate, w_up, w_down)


def _split_hi_lo(w):
    hi = w.astype(BF16)
    return hi, (w - hi.astype(F32)).astype(BF16)


def kernel(x_prompt, x_sample, cache_k_diff, cache_v_diff, state_retention, attn_norm_g, w_in, da_q_norm_g, da_k_norm_g, da_lambda_q1, da_lambda_k1, da_lambda_q2, da_lambda_k2, da_out_norm_g, ret_out_norm_g, w_out, ffn_norm_g, w_group, b_group, w_expert, b_expert, w_gate, w_up, w_down):
    assert w_in.shape[0] == 1, "single-layer model"
    bp, seq, d = x_prompt.shape
    bd, t_dec, _ = x_sample.shape
    past = cache_k_diff.shape[2]
    assert bp == 1

    w_in_b = w_in[0].astype(BF16)
    w_out_b = w_out[0].astype(BF16)
    w_gate_b = w_gate[0].astype(BF16)
    w_up_b = w_up[0].astype(BF16)
    w_down_b = w_down[0].astype(BF16)
    w_router = jnp.concatenate([w_expert[0].reshape(d, N_EXPERTS), w_group[0]], axis=1)
    w_router = jnp.pad(w_router, ((0, 0), (0, LANES - w_router.shape[1])))
    wr_hi, wr_lo = _split_hi_lo(w_router)
    b_router = jnp.concatenate([b_expert[0].reshape(N_EXPERTS), b_group[0]])
    b_router = jnp.pad(b_router, (0, LANES - b_router.shape[0])).reshape(1, LANES)

    lams = _lambda_args(da_lambda_q1[0], da_lambda_k1[0], da_lambda_q2[0], da_lambda_k2[0])

    def layer(x, cos, sin, attend, ret_seq, ret_chunk, s0, tm):
        h = _rmsnorm(x, attn_norm_g[0], tm)
        q_da, k_da, k_da_b, v_da, v_da_b, q_r, k_r, v_r, g_r = _project(
            h, w_in_b, da_q_norm_g[0], da_k_norm_g[0], cos, sin, tm)
        o_da = attend(q_da, k_da_b, v_da_b)
        o_ret, s_new = _retention(q_r, k_r, v_r, g_r, s0, ret_out_norm_g[0], ret_seq, ret_chunk)
        x1, hf, gate = _outproj(x, o_da, o_ret, w_out_b, ffn_norm_g[0], wr_hi, wr_lo, b_router,
                                min(tm, 256))
        y = _moe(hf, gate, x1, w_gate_b, w_up_b, w_down_b, tm)
        return y, k_da, v_da, s_new

    cos_p, sin_p = _rope_tables(seq, 0)
    y_p, k_p, v_p, s_p = layer(
        x_prompt.reshape(seq, d), cos_p, sin_p,
        lambda q, k, v: _da_prompt(q, k, v, lams, da_out_norm_g[0], 512),
        seq, 256, jnp.zeros((bp, RET_HEADS, RET_DK, RET_DV), F32), 512)

    cos_s, sin_s = _rope_tables(t_dec, past)
    cos_s = jnp.tile(cos_s, (bd, 1))
    sin_s = jnp.tile(sin_s, (bd, 1))
    ck = cache_k_diff[0].reshape(bd, past, DA_WIDTH)
    cv = cache_v_diff[0].reshape(bd, past, DA_WIDTH)
    y_s, k_s, v_s, s_s = layer(
        x_sample.reshape(bd * t_dec, d), cos_s, sin_s,
        lambda q, k, v: _da_sample(q, k, v, ck, cv, lams, da_out_norm_g[0], t_dec, 1024),
        t_dec, t_dec, state_retention[0], bd * t_dec)

    return (y_p.reshape(bp, seq, d),
            y_s.reshape(bd, t_dec, d),
            k_p.reshape(1, bp, seq, DA_HEADS, 2 * DA_DH),
            v_p.reshape(1, bp, seq, DA_HEADS, DA_DV),
            s_p.reshape(1, bp, RET_HEADS, RET_DK, RET_DV),
            k_s.reshape(1, bd, t_dec, DA_HEADS, 2 * DA_DH),
            v_s.reshape(1, bd, t_dec, DA_HEADS, DA_DV),
            s_s.reshape(1, bd, RET_HEADS, RET_DK, RET_DV))
```

```python
import functools
import math

import jax
import jax.numpy as jnp
from jax import lax
from jax.experimental import pallas as pl
from jax.experimental.pallas import tpu as pltpu

D_MODEL = 2048
CHUNK = 64
DA_HEADS = 4
DA_DH = 128
DA_DV = 2 * DA_DH
DA_WIDTH = DA_HEADS * DA_DV
RET_HEADS = 4
RET_DK = 256
RET_DV = 256
RET_WIDTH = RET_HEADS * RET_DV
IN_GROUP = 1024
N_GROUPS = 4
EXP_PER_GROUP = 8
N_EXPERTS = N_GROUPS * EXP_PER_GROUP
D_FF = D_MODEL // 8
EPS = 1e-6
NEG_INF = -1e30
ROPE_BASE = 10000.0
LAM_INIT = 0.8 - 0.6 * math.exp(-0.3 * 0)

LANES = 128
VMEM_LIMIT = 48 * 1024 * 1024

F32 = jnp.float32
BF16 = jnp.bfloat16


def _params(*sem):
    return pltpu.CompilerParams(dimension_semantics=sem, vmem_limit_bytes=VMEM_LIMIT)


def _sigmoid(x):
    return 1.0 / (1.0 + jnp.exp(-x))


def _rms(x):
    return x * lax.rsqrt(jnp.mean(x * x, axis=-1, keepdims=True) + EPS)


def _rmsnorm_body(x_ref, g_ref, o_ref):
    o_ref[...] = (_rms(x_ref[...]) * g_ref[...]).astype(o_ref.dtype)


def _rmsnorm(x, g, tm):
    t, d = x.shape
    return pl.pallas_call(
        _rmsnorm_body,
        out_shape=jax.ShapeDtypeStruct((t, d), BF16),
        grid=(t // tm,),
        in_specs=[pl.BlockSpec((tm, d), lambda i: (i, 0)),
                  pl.BlockSpec((1, d), lambda i: (0, 0))],
        out_specs=pl.BlockSpec((tm, d), lambda i: (i, 0)),
        compiler_params=_params("parallel"),
        name="attn_norm",
    )(x, g.reshape(1, d))


def _rope_table_body(pos0, tr, invf_ref, cos_ref, sin_ref):
    row = lax.broadcasted_iota(jnp.int32, (tr, LANES), 0) + (pl.program_id(0) * tr + pos0)
    ang = row.astype(F32) * invf_ref[...]
    cos_ref[...] = jnp.cos(ang)
    sin_ref[...] = jnp.sin(ang)


def _rope_tables(n_pos, pos0):
    half = RET_DK // 2
    inv_freq = (ROPE_BASE ** (-jnp.arange(half, dtype=F32) / half)).reshape(1, half)
    tr = min(n_pos, 512)
    spec = pl.BlockSpec((tr, half), lambda i: (i, 0))
    return pl.pallas_call(
        functools.partial(_rope_table_body, pos0, tr),
        out_shape=(jax.ShapeDtypeStruct((n_pos, half), F32),) * 2,
        grid=(n_pos // tr,),
        in_specs=[pl.BlockSpec((1, half), lambda i: (0, 0))],
        out_specs=(spec, spec),
        compiler_params=_params("parallel"),
        name="rope_tables",
    )(inv_freq)


def _store_cols(o, sl, val):
    if len(o.shape) == 2:
        o[:, sl] = val.astype(o.dtype)
    else:
        hd, off = divmod(sl.start, DA_DV)
        o[:, hd, off:off + (sl.stop - sl.start)] = val.astype(o.dtype)


def _store_all(z, outs):
    for o in outs:
        for hd in range(DA_HEADS):
            sl = slice(hd * DA_DV, (hd + 1) * DA_DV)
            _store_cols(o, sl, z[:, sl])


def _proj_plain_body(h_ref, w_ref, *outs):
    z = jnp.dot(h_ref[...], w_ref[...], preferred_element_type=F32)
    _store_all(z, outs)


def _proj_qknorm_body(scale, h_ref, w_ref, g_ref, *outs):
    z = jnp.dot(h_ref[...], w_ref[...], preferred_element_type=F32)
    g = g_ref[...]
    for c in range(IN_GROUP // DA_DH):
        sl = slice(c * DA_DH, (c + 1) * DA_DH)
        zc = _rms(z[:, sl]) * g
        for o in outs:
            _store_cols(o, sl, zc * scale if o.dtype == BF16 else zc)


def _proj_rotary_body(scale, h_ref, w_ref, cos_ref, sin_ref, o_ref):
    z = jnp.dot(h_ref[...], w_ref[...], preferred_element_type=F32) * scale
    cos = cos_ref[...]
    sin = sin_ref[...]
    half = RET_DK // 2
    for hd in range(RET_HEADS):
        x1 = z[:, hd * RET_DK: hd * RET_DK + half]
        x2 = z[:, hd * RET_DK + half: (hd + 1) * RET_DK]
        o_ref[:, hd * RET_DK: hd * RET_DK + half] = (x1 * cos - x2 * sin).astype(o_ref.dtype)
        o_ref[:, hd * RET_DK + half: (hd + 1) * RET_DK] = (x1 * sin + x2 * cos).astype(o_ref.dtype)


def _proj(h, w_in, group, body, extra, extra_specs, out_dtypes, tm, name, per_head=False):
    t, d = h.shape
    row = lambda i: (i, 0)
    split = lambda dt: per_head and dt == F32
    shape = lambda dt: (t, DA_HEADS, DA_DV) if split(dt) else (t, IN_GROUP)
    block = lambda dt: (pl.BlockSpec((tm, DA_HEADS, DA_DV), lambda i: (i, 0, 0)) if split(dt)
                        else pl.BlockSpec((tm, IN_GROUP), row))
    outs = tuple(jax.ShapeDtypeStruct(shape(dt), dt) for dt in out_dtypes)
    res = pl.pallas_call(
        body,
        out_shape=outs,
        grid=(t // tm,),
        in_specs=[pl.BlockSpec((tm, d), row),
                  pl.BlockSpec((d, IN_GROUP), lambda i: (0, group))] + extra_specs,
        out_specs=tuple(block(dt) for dt in out_dtypes),
        compiler_params=_params("parallel"),
        name=name,
    )(h, w_in, *extra)
    return res


def _project(h, w_in, q_norm_g, k_norm_g, cos, sin, tm):
    half = RET_DK // 2
    row = lambda i: (i, 0)
    gspec = [pl.BlockSpec((1, DA_DH), lambda i: (0, 0))]
    rspec = [pl.BlockSpec((tm, half), row), pl.BlockSpec((tm, half), row)]
    (q_da,) = _proj(h, w_in, 0, functools.partial(_proj_qknorm_body, DA_DH ** -0.5),
                    [q_norm_g.reshape(1, DA_DH)], gspec, [BF16], tm, "proj_q_da")
    k_da, k_da_b = _proj(h, w_in, 1, functools.partial(_proj_qknorm_body, 1.0),
                         [k_norm_g.reshape(1, DA_DH)], gspec, [F32, BF16], tm, "proj_k_da",
                         per_head=True)
    v_da, v_da_b = _proj(h, w_in, 2, _proj_plain_body, [], [], [F32, BF16], tm, "proj_v_da",
                         per_head=True)
    (q_r,) = _proj(h, w_in, 3, functools.partial(_proj_rotary_body, 1.0),
                   [cos, sin], rspec, [BF16], tm, "proj_q_ret")
    (k_r,) = _proj(h, w_in, 4, functools.partial(_proj_rotary_body, RET_DK ** -0.5),
                   [cos, sin], rspec, [BF16], tm, "proj_k_ret")
    (v_r,) = _proj(h, w_in, 5, _proj_plain_body, [], [], [BF16], tm, "proj_v_ret")
    (g_r,) = _proj(h, w_in, 6, _proj_plain_body, [], [], [F32], tm, "proj_g_ret")
    return q_da, k_da, k_da_b, v_da, v_da_b, q_r, k_r, v_r, g_r


def _diff_lambda(lq1, lk1, lq2, lk2):
    s1 = jnp.sum(lq1[...] * lk1[...], axis=-1, keepdims=True)
    s2 = jnp.sum(lq2[...] * lk2[...], axis=-1, keepdims=True)
    return jnp.exp(s1) - jnp.exp(s2) + LAM_INIT


def _softmax_step(q, k, v, m_ref, l_ref, acc_ref, idx, mask):
    s = lax.dot_general(q, k, (((1,), (1,)), ((), ())), preferred_element_type=F32)
    if mask is not None:
        s = jnp.where(mask, s, NEG_INF)
    m_prev = m_ref[idx]
    m_new = jnp.maximum(m_prev, jnp.max(s, axis=-1, keepdims=True))
    alpha = jnp.exp(m_prev - m_new)
    p = jnp.exp(s - m_new)
    l_ref[idx] = alpha * l_ref[idx] + jnp.sum(p, axis=-1, keepdims=True)
    acc_ref[idx] = alpha * acc_ref[idx] + jnp.dot(p.astype(BF16), v, preferred_element_type=F32)
    m_ref[idx] = m_new


def _softmax_init(m_ref, l_ref, acc_ref):
    m_ref[...] = jnp.full(m_ref.shape, NEG_INF, F32)
    l_ref[...] = jnp.zeros(l_ref.shape, F32)
    acc_ref[...] = jnp.zeros(acc_ref.shape, F32)


def _diff_combine(lam, gain, l_ref, acc_ref, i0, i1):
    o = acc_ref[i0] / l_ref[i0] - lam * (acc_ref[i1] / l_ref[i1])
    return _rms(o) * gain * (1.0 - LAM_INIT)


def _da_prompt_body(tq, lq1, lk1, lq2, lk2, gain_ref, q_ref, k_ref, v_ref, o_ref,
                    m_ref, l_ref, acc_ref):
    qi = pl.program_id(1)
    _softmax_init(m_ref, l_ref, acc_ref)

    def block(start, mask):
        k = k_ref[pl.ds(start, tq), :]
        v = v_ref[pl.ds(start, tq), :]
        for mp in range(2):
            sl = slice(mp * DA_DH, (mp + 1) * DA_DH)
            _softmax_step(q_ref[:, sl], k[:, sl], v, m_ref, l_ref, acc_ref, mp, mask)

    def full_block(kb, carry):
        block(pl.multiple_of(kb * tq, tq), None)
        return carry

    lax.fori_loop(0, qi, full_block, 0)
    rc = lax.shift_right_logical(lax.broadcasted_iota(jnp.int32, (tq, tq), 0), 6)
    cc = lax.shift_right_logical(lax.broadcasted_iota(jnp.int32, (tq, tq), 1), 6)
    block(pl.multiple_of(qi * tq, tq), cc <= rc)

    lam = _diff_lambda(lq1, lk1, lq2, lk2)
    o_ref[...] = _diff_combine(lam, gain_ref[...], l_ref, acc_ref, 0, 1).astype(o_ref.dtype)


def _lambda_args(lq1, lk1, lq2, lk2):
    return [a.reshape(1, DA_DH) for a in (lq1, lk1, lq2, lk2)]


def _da_prompt(q, k, v, lams, out_gain, tq):
    s = q.shape[0]
    assert CHUNK == 64 and tq % CHUNK == 0 and s % tq == 0
    const = lambda h, i: (0, 0)
    vec = pl.BlockSpec((1, DA_DH), const)
    return pl.pallas_call(
        functools.partial(_da_prompt_body, tq),
        out_shape=jax.ShapeDtypeStruct((s, DA_WIDTH), BF16),
        grid=(DA_HEADS, s // tq),
        in_specs=[vec, vec, vec, vec,
                  pl.BlockSpec((1, DA_DV), const),
                  pl.BlockSpec((tq, DA_DV), lambda h, i: (i, h)),
                  pl.BlockSpec((s, DA_DV), lambda h, i: (0, h)),
                  pl.BlockSpec((s, DA_DV), lambda h, i: (0, h))],
        out_specs=pl.BlockSpec((tq, DA_DV), lambda h, i: (i, h)),
        scratch_shapes=[pltpu.VMEM((2, tq, 1), F32), pltpu.VMEM((2, tq, 1), F32),
                        pltpu.VMEM((2, tq, DA_DV), F32)],
        compiler_params=_params("parallel", "parallel"),
        name="diff_attn_prompt",
    )(*lams, out_gain.reshape(1, DA_DV), q, k, v)


def _da_sample_body(tk, lq1, lk1, lq2, lk2, gain_ref, q_ref, kn_ref, vn_ref, ck_hbm, cv_hbm, o_ref,
                    kbuf, vbuf, sem, m_ref, l_ref, acc_ref):
    n = pl.program_id(0) * DA_HEADS + pl.program_id(1)
    total = pl.num_programs(0) * DA_HEADS
    slot = lax.rem(n, 2)

    def copies(step, to_slot):
        stream = lax.div(step, DA_HEADS)
        head = lax.rem(step, DA_HEADS)
        return (pltpu.make_async_copy(ck_hbm.at[stream, :, head, :], kbuf.at[to_slot], sem.at[0, to_slot]),
                pltpu.make_async_copy(cv_hbm.at[stream, :, head, :], vbuf.at[to_slot], sem.at[1, to_slot]))

    @pl.when(n == 0)
    def _():
        for c in copies(n, slot):
            c.start()

    @pl.when(n + 1 < total)
    def _():
        for c in copies(n + 1, 1 - slot):
            c.start()

    def attend(k, v):
        for mp in range(2):
            sl = slice(mp * DA_DH, (mp + 1) * DA_DH)
            _softmax_step(q_ref[:, sl], k[:, sl], v, m_ref, l_ref, acc_ref, mp, None)

    _softmax_init(m_ref, l_ref, acc_ref)
    attend(kn_ref[...], vn_ref[...])
    for c in copies(n, slot):
        c.wait()
    for j in range(kbuf.shape[1] // tk):
        attend(kbuf[slot, j * tk:(j + 1) * tk, :].astype(BF16),
               vbuf[slot, j * tk:(j + 1) * tk, :].astype(BF16))
    lam = _diff_lambda(lq1, lk1, lq2, lk2)
    o_ref[...] = _diff_combine(lam, gain_ref[...], l_ref, acc_ref, 0, 1).astype(o_ref.dtype)


def _da_sample(q, k_new, v_new, cache_k, cache_v, lams, out_gain, t, tk):
    b, p = cache_k.shape[:2]
    const = lambda i, h: (0, 0)
    vec = pl.BlockSpec((1, DA_DH), const)
    tok = pl.BlockSpec((t, DA_DV), lambda i, h: (i, h))
    hbm = pl.BlockSpec(memory_space=pl.ANY)
    return pl.pallas_call(
        functools.partial(_da_sample_body, tk),
        out_shape=jax.ShapeDtypeStruct((b * t, DA_WIDTH), BF16),
        grid=(b, DA_HEADS),
        in_specs=[vec, vec, vec, vec, pl.BlockSpec((1, DA_DV), const), tok, tok, tok, hbm, hbm],
        out_specs=tok,
        scratch_shapes=[pltpu.VMEM((2, p, DA_DV), F32), pltpu.VMEM((2, p, DA_DV), F32),
                        pltpu.SemaphoreType.DMA((2, 2)),
                        pltpu.VMEM((2, t, 1), F32), pltpu.VMEM((2, t, 1), F32),
                        pltpu.VMEM((2, t, DA_DV), F32)],
        compiler_params=_params("arbitrary", "arbitrary"),
        name="diff_attn_sample",
    )(*lams, out_gain.reshape(1, DA_DV), q, k_new, v_new, cache_k, cache_v)


def _retention_body(lc, lg_ref, gain_ref, q_ref, k_ref, v_ref, g_ref, s0_ref, o_ref, s_out_ref,
                    state_ref):
    c = pl.program_id(2)

    @pl.when(c == 0)
    def _():
        state_ref[...] = s0_ref[0, 0]

    lg = lg_ref[0][:, :1]
    q = q_ref[...]
    k = k_ref[...]
    v = v_ref[...]
    rel = (lax.broadcasted_iota(jnp.int32, (lc, lc), 0)
           - lax.broadcasted_iota(jnp.int32, (lc, lc), 1)).astype(F32)
    decay = jnp.where(rel >= 0, jnp.exp(lg * jnp.maximum(rel, 0.0)), 0.0)
    scores = lax.dot_general(q, k, (((1,), (1,)), ((), ())), preferred_element_type=F32) * decay
    idx = lax.broadcasted_iota(jnp.int32, (lc, 1), 0).astype(F32)
    state = state_ref[...]
    o = jnp.dot(scores.astype(BF16), v, preferred_element_type=F32)
    o = o + jnp.dot(q, state.astype(BF16), preferred_element_type=F32) * jnp.exp(lg * (idx + 1.0))
    kw = k.astype(F32) * jnp.exp(lg * (lc - 1.0 - idx))
    s_new = state * jnp.exp(lg * lc) + jnp.dot(kw.T.astype(BF16), v, preferred_element_type=F32)
    state_ref[...] = s_new

    g = g_ref[...]
    o_ref[...] = (_rms(o) * gain_ref[...] * (g * _sigmoid(g))).astype(o_ref.dtype)

    @pl.when(c == pl.num_programs(2) - 1)
    def _():
        s_out_ref[0, 0] = s_new


def _retention(q, k, v, g, s0, out_gain, seq, lc):
    b = s0.shape[0]
    nc = seq // lc
    log_gamma = jnp.log(1.0 - 2.0 ** (-5.0 - jnp.arange(RET_HEADS, dtype=F32)))
    lg = jnp.broadcast_to(log_gamma.reshape(RET_HEADS, 1, 1), (RET_HEADS, 1, LANES))
    tok = pl.BlockSpec((lc, RET_DV), lambda bi, h, c: (bi * nc + c, h))
    st = pl.BlockSpec((1, 1, RET_DK, RET_DV), lambda bi, h, c: (bi, h, 0, 0))
    return pl.pallas_call(
        functools.partial(_retention_body, lc),
        out_shape=(jax.ShapeDtypeStruct((b * seq, RET_WIDTH), BF16),
                   jax.ShapeDtypeStruct(s0.shape, F32)),
        grid=(b, RET_HEADS, nc),
        in_specs=[pl.BlockSpec((1, 1, LANES), lambda bi, h, c: (h, 0, 0)),
                  pl.BlockSpec((1, RET_DV), lambda bi, h, c: (0, 0)),
                  tok, tok, tok, tok, st],
        out_specs=(tok, st),
        scratch_shapes=[pltpu.VMEM((RET_DK, RET_DV), F32)],
        compiler_params=_params("parallel", "parallel", "arbitrary"),
        name="retention",
    )(lg, out_gain.reshape(1, RET_DV), q, k, v, g, s0)


def _route(logits):
    lane = lax.broadcasted_iota(jnp.int32, logits.shape, 1)
    big = jnp.int32(LANES)
    neg = -jnp.inf
    gl = jnp.where((lane >= N_EXPERTS) & (lane < N_EXPERTS + N_GROUPS), logits, neg)
    g_max = jnp.max(gl, axis=-1, keepdims=True)
    g_idx = jnp.min(jnp.where(gl == g_max, lane - N_EXPERTS, big), axis=-1, keepdims=True)
    g_w = 1.0 / jnp.sum(jnp.exp(gl - g_max), axis=-1, keepdims=True)
    in_group = (lane < N_EXPERTS) & (lax.shift_right_logical(lane, 3) == g_idx)
    el = jnp.where(in_group, logits, neg)
    v1 = jnp.max(el, axis=-1, keepdims=True)
    i1 = jnp.min(jnp.where(el == v1, lane, big), axis=-1, keepdims=True)
    el2 = jnp.where(lane == i1, neg, el)
    v2 = jnp.max(el2, axis=-1, keepdims=True)
    i2 = jnp.min(jnp.where(el2 == v2, lane, big), axis=-1, keepdims=True)
    e2 = jnp.exp(v2 - v1)
    w1 = g_w / (1.0 + e2)
    w2 = g_w * e2 / (1.0 + e2)
    return jnp.where(lane == i1, w1, 0.0) + jnp.where(lane == i2, w2, 0.0)


def _outproj_body(x_ref, oda_ref, ort_ref, wo_ref, gf_ref, wr_hi_ref, wr_lo_ref, br_ref,
                  x1_ref, hf_ref, gate_ref):
    x1 = (x_ref[...]
          + jnp.dot(oda_ref[...], wo_ref[:DA_WIDTH, :], preferred_element_type=F32)
          + jnp.dot(ort_ref[...], wo_ref[DA_WIDTH:, :], preferred_element_type=F32))
    x1_ref[...] = x1
    hf = _rms(x1) * gf_ref[...]
    hf_hi = hf.astype(BF16)
    hf_ref[...] = hf_hi
    hf_lo = (hf - hf_hi.astype(F32)).astype(BF16)
    wr_hi = wr_hi_ref[...]
    logits = (jnp.dot(hf_hi, wr_hi, preferred_element_type=F32)
              + jnp.dot(hf_lo, wr_hi, preferred_element_type=F32)
              + jnp.dot(hf_hi, wr_lo_ref[...], preferred_element_type=F32)
              + br_ref[...])
    gate_ref[...] = _route(logits)


def _outproj(x, o_da, o_ret, w_out, ffn_g, wr_hi, wr_lo, br, tm):
    t, d = x.shape
    row = lambda i: (i, 0)
    const = lambda i: (0, 0)
    return pl.pallas_call(
        _outproj_body,
        out_shape=(jax.ShapeDtypeStruct((t, d), F32), jax.ShapeDtypeStruct((t, d), BF16),
                   jax.ShapeDtypeStruct((t, LANES), F32)),
        grid=(t // tm,),
        in_specs=[pl.BlockSpec((tm, d), row),
                  pl.BlockSpec((tm, DA_WIDTH), row),
                  pl.BlockSpec((tm, RET_WIDTH), row),
                  pl.BlockSpec((d, d), const),
                  pl.BlockSpec((1, d), const),
                  pl.BlockSpec((d, LANES), const),
                  pl.BlockSpec((d, LANES), const),
                  pl.BlockSpec((1, LANES), const)],
        out_specs=(pl.BlockSpec((tm, d), row), pl.BlockSpec((tm, d), row),
                   pl.BlockSpec((tm, LANES), row)),
        compiler_params=_params("parallel"),
        name="out_proj_router",
    )(x, o_da, o_ret, w_out, ffn_g.reshape(1, d), wr_hi, wr_lo, br)


def _moe_body(hf_ref, gate_ref, x1_ref, wg_ref, wu_ref, wd_ref, y_ref):
    e = pl.program_id(1)

    @pl.when(e == 0)
    def _():
        y_ref[...] = x1_ref[...]

    h = hf_ref[...]
    a = jnp.dot(h, wg_ref[0], preferred_element_type=F32)
    u = jnp.dot(h, wu_ref[0], preferred_element_type=F32)
    gate = gate_ref[...]
    lane = lax.broadcasted_iota(jnp.int32, gate.shape, 1)
    ge = jnp.sum(jnp.where(lane == e, gate, 0.0), axis=-1, keepdims=True)
    act = (a * _sigmoid(a)) * u * ge
    y_ref[...] += jnp.dot(act.astype(BF16), wd_ref[0], preferred_element_type=F32)


def _moe(hf, gate, x1, w_gate, w_up, w_down, tm):
    t, d = hf.shape
    row = lambda i, e: (i, 0)
    return pl.pallas_call(
        _moe_body,
        out_shape=jax.ShapeDtypeStruct((t, d), F32),
        grid=(t // tm, N_EXPERTS),
        in_specs=[pl.BlockSpec((tm, d), row),
                  pl.BlockSpec((tm, LANES), row),
                  pl.BlockSpec((tm, d), row),
                  pl.BlockSpec((1, d, D_FF), lambda i, e: (e, 0, 0)),
                  pl.BlockSpec((1, d, D_FF), lambda i, e: (e, 0, 0)),
                  pl.BlockSpec((1, D_FF, d), lambda i, e: (e, 0, 0))],
        out_specs=pl.BlockSpec((tm, d), row),
        compiler_params=_params("parallel", "arbitrary"),
        name="moe",
    )(hf, gate, x1, w_gate, w_up, w_down)


def _split_hi_lo(w):
    hi = w.astype(BF16)
    return hi, (w - hi.astype(F32)).astype(BF16)


def kernel(x_prompt, x_sample, cache_k_diff, cache_v_diff, state_retention, attn_norm_g, w_in, da_q_norm_g, da_k_norm_g, da_lambda_q1, da_lambda_k1, da_lambda_q2, da_lambda_k2, da_out_norm_g, ret_out_norm_g, w_out, ffn_norm_g, w_group, b_group, w_expert, b_expert, w_gate, w_up, w_down):
    assert w_in.shape[0] == 1, "single-layer model"
    bp, seq, d = x_prompt.shape
    bd, t_dec, _ = x_sample.shape
    past = cache_k_diff.shape[2]
    assert bp == 1

    w_in_b = w_in[0].astype(BF16)
    w_out_b = w_out[0].astype(BF16)
    w_gate_b = w_gate[0].astype(BF16)
    w_up_b = w_up[0].astype(BF16)
    w_down_b = w_down[0].astype(BF16)
    w_router = jnp.concatenate([w_expert[0].reshape(d, N_EXPERTS), w_group[0]], axis=1)
    w_router = jnp.pad(w_router, ((0, 0), (0, LANES - w_router.shape[1])))
    wr_hi, wr_lo = _split_hi_lo(w_router)
    b_router = jnp.concatenate([b_expert[0].reshape(N_EXPERTS), b_group[0]])
    b_router = jnp.pad(b_router, (0, LANES - b_router.shape[0])).reshape(1, LANES)

    lams = _lambda_args(da_lambda_q1[0], da_lambda_k1[0], da_lambda_q2[0], da_lambda_k2[0])

    def layer(x, cos, sin, attend, ret_seq, ret_chunk, s0, tm):
        h = _rmsnorm(x, attn_norm_g[0], tm)
        q_da, k_da, k_da_b, v_da, v_da_b, q_r, k_r, v_r, g_r = _project(
            h, w_in_b, da_q_norm_g[0], da_k_norm_g[0], cos, sin, tm)
        o_da = attend(q_da, k_da_b, v_da_b)
        o_ret, s_new = _retention(q_r, k_r, v_r, g_r, s0, ret_out_norm_g[0], ret_seq, ret_chunk)
        x1, hf, gate = _outproj(x, o_da, o_ret, w_out_b, ffn_norm_g[0], wr_hi, wr_lo, b_router,
                                min(tm, 256))
        y = _moe(hf, gate, x1, w_gate_b, w_up_b, w_down_b, tm)
        return y, k_da, v_da, s_new

    cos_p, sin_p = _rope_tables(seq, 0)
    y_p, k_p, v_p, s_p = layer(
        x_prompt.reshape(seq, d), cos_p, sin_p,
        lambda q, k, v: _da_prompt(q, k, v, lams, da_out_norm_g[0], 512),
        seq, 256, jnp.zeros((bp, RET_HEADS, RET_DK, RET_DV), F32), 512)

    cos_s, sin_s = _rope_tables(t_dec, past)
    cos_s = jnp.tile(cos_s, (bd, 1))
    sin_s = jnp.tile(sin_s, (bd, 1))
    y_s, k_s, v_s, s_s = layer(
        x_sample.reshape(bd * t_dec, d), cos_s, sin_s,
        lambda q, k, v: _da_sample(q, k, v, cache_k_diff[0], cache_v_diff[0], lams,
                                   da_out_norm_g[0], t_dec, 1024),
        t_dec, t_dec, state_retention[0], bd * t_dec)

    return (y_p.reshape(bp, seq, d),
            y_s.reshape(bd, t_dec, d),
            k_p.reshape(1, bp, seq, DA_HEADS, 2 * DA_DH),
            v_p.reshape(1, bp, seq, DA_HEADS, DA_DV),
            s_p.reshape(1, bp, RET_HEADS, RET_DK, RET_DV),
            k_s.reshape(1, bd, t_dec, DA_HEADS, 2 * DA_DH),
            v_s.reshape(1, bd, t_dec, DA_HEADS, DA_DV),
            s_s.reshape(1, bd, RET_HEADS, RET_DK, RET_DV))
```

```python
import functools
import math

import jax
import jax.numpy as jnp
from jax import lax
from jax.experimental import pallas as pl
from jax.experimental.pallas import tpu as pltpu

D_MODEL = 2048
CHUNK = 64
DA_HEADS = 4
DA_DH = 128
DA_DV = 2 * DA_DH
DA_WIDTH = DA_HEADS * DA_DV
RET_HEADS = 4
RET_DK = 256
RET_DV = 256
RET_WIDTH = RET_HEADS * RET_DV
IN_GROUP = 1024
N_GROUPS = 4
EXP_PER_GROUP = 8
N_EXPERTS = N_GROUPS * EXP_PER_GROUP
D_FF = D_MODEL // 8
EPS = 1e-6
NEG_INF = -1e30
ROPE_BASE = 10000.0
LAM_INIT = 0.8 - 0.6 * math.exp(-0.3 * 0)

LANES = 128
VMEM_LIMIT = 48 * 1024 * 1024

F32 = jnp.float32
BF16 = jnp.bfloat16


def _params(*sem):
    return pltpu.CompilerParams(dimension_semantics=sem, vmem_limit_bytes=VMEM_LIMIT)


def _sigmoid(x):
    return 1.0 / (1.0 + jnp.exp(-x))


def _rms(x):
    return x * lax.rsqrt(jnp.mean(x * x, axis=-1, keepdims=True) + EPS)


def _rmsnorm_body(x_ref, g_ref, o_ref):
    o_ref[...] = (_rms(x_ref[...]) * g_ref[...]).astype(o_ref.dtype)


def _rmsnorm(x, g, tm):
    t, d = x.shape
    return pl.pallas_call(
        _rmsnorm_body,
        out_shape=jax.ShapeDtypeStruct((t, d), BF16),
        grid=(t // tm,),
        in_specs=[pl.BlockSpec((tm, d), lambda i: (i, 0)),
                  pl.BlockSpec((1, d), lambda i: (0, 0))],
        out_specs=pl.BlockSpec((tm, d), lambda i: (i, 0)),
        compiler_params=_params("parallel"),
        name="attn_norm",
    )(x, g.reshape(1, d))


def _rope_table_body(pos0, tr, invf_ref, cos_ref, sin_ref):
    row = lax.broadcasted_iota(jnp.int32, (tr, LANES), 0) + (pl.program_id(0) * tr + pos0)
    ang = row.astype(F32) * invf_ref[...]
    cos_ref[...] = jnp.cos(ang)
    sin_ref[...] = jnp.sin(ang)


def _rope_tables(n_pos, pos0):
    half = RET_DK // 2
    inv_freq = (ROPE_BASE ** (-jnp.arange(half, dtype=F32) / half)).reshape(1, half)
    tr = min(n_pos, 512)
    spec = pl.BlockSpec((tr, half), lambda i: (i, 0))
    return pl.pallas_call(
        functools.partial(_rope_table_body, pos0, tr),
        out_shape=(jax.ShapeDtypeStruct((n_pos, half), F32),) * 2,
        grid=(n_pos // tr,),
        in_specs=[pl.BlockSpec((1, half), lambda i: (0, 0))],
        out_specs=(spec, spec),
        compiler_params=_params("parallel"),
        name="rope_tables",
    )(inv_freq)


def _store_cols(o, sl, val):
    if len(o.shape) == 2:
        o[:, sl] = val.astype(o.dtype)
    else:
        hd, off = divmod(sl.start, DA_DV)
        o[:, hd, off:off + (sl.stop - sl.start)] = val.astype(o.dtype)


def _store_all(z, outs):
    for o in outs:
        for hd in range(DA_HEADS):
            sl = slice(hd * DA_DV, (hd + 1) * DA_DV)
            _store_cols(o, sl, z[:, sl])


def _proj_plain_body(h_ref, w_ref, *outs):
    z = jnp.dot(h_ref[...], w_ref[...], preferred_element_type=F32)
    _store_all(z, outs)


def _proj_plain_t_body(h_ref, w_ref, *outs):
    z = jnp.dot(h_ref[...], w_ref[...], preferred_element_type=F32)
    _store_all(z, outs[:-1])
    outs[-1][...] = z.T.astype(outs[-1].dtype)


def _proj_qknorm_body(scale, h_ref, w_ref, g_ref, *outs):
    z = jnp.dot(h_ref[...], w_ref[...], preferred_element_type=F32)
    g = g_ref[...]
    for c in range(IN_GROUP // DA_DH):
        sl = slice(c * DA_DH, (c + 1) * DA_DH)
        zc = _rms(z[:, sl]) * g
        for o in outs:
            _store_cols(o, sl, zc * scale if o.dtype == BF16 else zc)


def _proj_rotary_body(scale, h_ref, w_ref, cos_ref, sin_ref, o_ref):
    z = jnp.dot(h_ref[...], w_ref[...], preferred_element_type=F32) * scale
    cos = cos_ref[...]
    sin = sin_ref[...]
    half = RET_DK // 2
    for hd in range(RET_HEADS):
        x1 = z[:, hd * RET_DK: hd * RET_DK + half]
        x2 = z[:, hd * RET_DK + half: (hd + 1) * RET_DK]
        o_ref[:, hd * RET_DK: hd * RET_DK + half] = (x1 * cos - x2 * sin).astype(o_ref.dtype)
        o_ref[:, hd * RET_DK + half: (hd + 1) * RET_DK] = (x1 * sin + x2 * cos).astype(o_ref.dtype)


def _proj(h, w_in, group, body, extra, extra_specs, out_dtypes, tm, name, per_head=False,
          transposed_out=False):
    t, d = h.shape
    row = lambda i: (i, 0)
    split = lambda dt: per_head and dt == F32
    shape = lambda dt: (t, DA_HEADS, DA_DV) if split(dt) else (t, IN_GROUP)
    block = lambda dt: (pl.BlockSpec((tm, DA_HEADS, DA_DV), lambda i: (i, 0, 0)) if split(dt)
                        else pl.BlockSpec((tm, IN_GROUP), row))
    outs = tuple(jax.ShapeDtypeStruct(shape(dt), dt) for dt in out_dtypes)
    out_specs = tuple(block(dt) for dt in out_dtypes)
    if transposed_out:
        outs += (jax.ShapeDtypeStruct((IN_GROUP, t), BF16),)
        out_specs += (pl.BlockSpec((IN_GROUP, tm), lambda i: (0, i)),)
    res = pl.pallas_call(
        body,
        out_shape=outs,
        grid=(t // tm,),
        in_specs=[pl.BlockSpec((tm, d), row),
                  pl.BlockSpec((d, IN_GROUP), lambda i: (0, group))] + extra_specs,
        out_specs=out_specs,
        compiler_params=_params("parallel"),
        name=name,
    )(h, w_in, *extra)
    return res


def _project(h, w_in, q_norm_g, k_norm_g, cos, sin, tm, want_vt):
    half = RET_DK // 2
    row = lambda i: (i, 0)
    gspec = [pl.BlockSpec((1, DA_DH), lambda i: (0, 0))]
    rspec = [pl.BlockSpec((tm, half), row), pl.BlockSpec((tm, half), row)]
    (q_da,) = _proj(h, w_in, 0, functools.partial(_proj_qknorm_body, DA_DH ** -0.5),
                    [q_norm_g.reshape(1, DA_DH)], gspec, [BF16], tm, "proj_q_da")
    k_da, k_da_b = _proj(h, w_in, 1, functools.partial(_proj_qknorm_body, 1.0),
                         [k_norm_g.reshape(1, DA_DH)], gspec, [F32, BF16], tm, "proj_k_da",
                         per_head=True)
    v_da, v_da_b, *v_da_t = _proj(h, w_in, 2, _proj_plain_t_body if want_vt else _proj_plain_body,
                                  [], [], [F32, BF16], tm, "proj_v_da", per_head=True,
                                  transposed_out=want_vt)
    (q_r,) = _proj(h, w_in, 3, functools.partial(_proj_rotary_body, 1.0),
                   [cos, sin], rspec, [BF16], tm, "proj_q_ret")
    (k_r,) = _proj(h, w_in, 4, functools.partial(_proj_rotary_body, RET_DK ** -0.5),
                   [cos, sin], rspec, [BF16], tm, "proj_k_ret")
    (v_r,) = _proj(h, w_in, 5, _proj_plain_body, [], [], [BF16], tm, "proj_v_ret")
    (g_r,) = _proj(h, w_in, 6, _proj_plain_body, [], [], [F32], tm, "proj_g_ret")
    return q_da, k_da, k_da_b, v_da, v_da_b, v_da_t, q_r, k_r, v_r, g_r


def _diff_lambda(lq1, lk1, lq2, lk2):
    s1 = jnp.sum(lq1[...] * lk1[...], axis=-1, keepdims=True)
    s2 = jnp.sum(lq2[...] * lk2[...], axis=-1, keepdims=True)
    return jnp.exp(s1) - jnp.exp(s2) + LAM_INIT


def _softmax_step(q, k, v, m_ref, l_ref, acc_ref, idx, mask):
    s = lax.dot_general(q, k, (((1,), (1,)), ((), ())), preferred_element_type=F32)
    if mask is not None:
        s = jnp.where(mask, s, NEG_INF)
    m_prev = m_ref[idx]
    m_new = jnp.maximum(m_prev, jnp.max(s, axis=-1, keepdims=True))
    alpha = jnp.exp(m_prev - m_new)
    p = jnp.exp(s - m_new)
    l_ref[idx] = alpha * l_ref[idx] + jnp.sum(p, axis=-1, keepdims=True)
    acc_ref[idx] = alpha * acc_ref[idx] + jnp.dot(p.astype(BF16), v, preferred_element_type=F32)
    m_ref[idx] = m_new


def _softmax_init(m_ref, l_ref, acc_ref):
    m_ref[...] = jnp.full(m_ref.shape, NEG_INF, F32)
    l_ref[...] = jnp.zeros(l_ref.shape, F32)
    acc_ref[...] = jnp.zeros(acc_ref.shape, F32)


def _diff_combine(lam, gain, l_ref, acc_ref, i0, i1):
    o = acc_ref[i0] / l_ref[i0] - lam * (acc_ref[i1] / l_ref[i1])
    return _rms(o) * gain * (1.0 - LAM_INIT)


def _da_prompt_body(tq, lq1, lk1, lq2, lk2, gain_ref, q_ref, k_ref, vt_ref, o_ref,
                    m_ref, l_ref, acc_ref):
    qi = pl.program_id(1)
    _softmax_init(m_ref, l_ref, acc_ref)

    def block(start, mask):
        k = k_ref[pl.ds(start, tq), :]
        vt = vt_ref[:, pl.ds(start, tq)]
        sts = []
        for mp in range(2):
            sl = slice(mp * DA_DH, (mp + 1) * DA_DH)
            sts.append(lax.dot_general(k[:, sl], q_ref[:, sl], (((1,), (1,)), ((), ())),
                                       preferred_element_type=F32))
        pts, alphas = [], []
        for mp in range(2):
            st = sts[mp]
            if mask is not None:
                st = jnp.where(mask, st, NEG_INF)
            m_prev = m_ref[mp]
            m_new = jnp.maximum(m_prev, jnp.max(st, axis=0, keepdims=True))
            alpha = jnp.exp(m_prev - m_new)
            pt = jnp.exp(st - m_new)
            l_ref[mp] = alpha * l_ref[mp] + jnp.sum(pt, axis=0, keepdims=True)
            m_ref[mp] = m_new
            pts.append(pt.astype(BF16))
            alphas.append(alpha)
        for mp in range(2):
            acc_ref[mp] = alphas[mp] * acc_ref[mp] + jnp.dot(vt, pts[mp],
                                                             preferred_element_type=F32)

    def full_block(kb, carry):
        block(pl.multiple_of(kb * tq, tq), None)
        return carry

    lax.fori_loop(0, qi, full_block, 0)
    kc = lax.shift_right_logical(lax.broadcasted_iota(jnp.int32, (tq, tq), 0), 6)
    qc = lax.shift_right_logical(lax.broadcasted_iota(jnp.int32, (tq, tq), 1), 6)
    block(pl.multiple_of(qi * tq, tq), kc <= qc)

    lam = _diff_lambda(lq1, lk1, lq2, lk2)
    ot = acc_ref[0] / l_ref[0] - lam * (acc_ref[1] / l_ref[1])
    ot = ot * lax.rsqrt(jnp.mean(ot * ot, axis=0, keepdims=True) + EPS)
    ot = ot * gain_ref[...] * (1.0 - LAM_INIT)
    o_ref[...] = ot.T.astype(o_ref.dtype)


def _lambda_args(lq1, lk1, lq2, lk2):
    return [a.reshape(1, DA_DH) for a in (lq1, lk1, lq2, lk2)]


def _da_prompt(q, k, vt, lams, out_gain, tq):
    s = q.shape[0]
    assert CHUNK == 64 and tq % CHUNK == 0 and s % tq == 0
    const = lambda h, i: (0, 0)
    vec = pl.BlockSpec((1, DA_DH), const)
    return pl.pallas_call(
        functools.partial(_da_prompt_body, tq),
        out_shape=jax.ShapeDtypeStruct((s, DA_WIDTH), BF16),
        grid=(DA_HEADS, s // tq),
        in_specs=[vec, vec, vec, vec,
                  pl.BlockSpec((DA_DV, 1), const),
                  pl.BlockSpec((tq, DA_DV), lambda h, i: (i, h)),
                  pl.BlockSpec((s, DA_DV), lambda h, i: (0, h)),
                  pl.BlockSpec((DA_DV, s), lambda h, i: (h, 0))],
        out_specs=pl.BlockSpec((tq, DA_DV), lambda h, i: (i, h)),
        scratch_shapes=[pltpu.VMEM((2, 1, tq), F32), pltpu.VMEM((2, 1, tq), F32),
                        pltpu.VMEM((2, DA_DV, tq), F32)],
        compiler_params=_params("parallel", "parallel"),
        name="diff_attn_prompt",
    )(*lams, out_gain.reshape(DA_DV, 1), q, k, vt)


def _da_sample_body(tk, lq1, lk1, lq2, lk2, gain_ref, q_ref, kn_ref, vn_ref, ck_hbm, cv_hbm, o_ref,
                    kbuf, vbuf, sem, m_ref, l_ref, acc_ref):
    n = pl.program_id(0) * DA_HEADS + pl.program_id(1)
    total = pl.num_programs(0) * DA_HEADS
    slot = lax.rem(n, 2)

    def copies(step, to_slot):
        stream = lax.div(step, DA_HEADS)
        head = lax.rem(step, DA_HEADS)
        return (pltpu.make_async_copy(ck_hbm.at[stream, :, head, :], kbuf.at[to_slot], sem.at[0, to_slot]),
                pltpu.make_async_copy(cv_hbm.at[stream, :, head, :], vbuf.at[to_slot], sem.at[1, to_slot]))

    @pl.when(n == 0)
    def _():
        for c in copies(n, slot):
            c.start()

    @pl.when(n + 1 < total)
    def _():
        for c in copies(n + 1, 1 - slot):
            c.start()

    def attend(k, v):
        for mp in range(2):
            sl = slice(mp * DA_DH, (mp + 1) * DA_DH)
            _softmax_step(q_ref[:, sl], k[:, sl], v, m_ref, l_ref, acc_ref, mp, None)

    _softmax_init(m_ref, l_ref, acc_ref)
    attend(kn_ref[...], vn_ref[...])
    for c in copies(n, slot):
        c.wait()
    for j in range(kbuf.shape[1] // tk):
        attend(kbuf[slot, j * tk:(j + 1) * tk, :].astype(BF16),
               vbuf[slot, j * tk:(j + 1) * tk, :].astype(BF16))
    lam = _diff_lambda(lq1, lk1, lq2, lk2)
    o_ref[...] = _diff_combine(lam, gain_ref[...], l_ref, acc_ref, 0, 1).astype(o_ref.dtype)


def _da_sample(q, k_new, v_new, cache_k, cache_v, lams, out_gain, t, tk):
    b, p = cache_k.shape[:2]
    const = lambda i, h: (0, 0)
    vec = pl.BlockSpec((1, DA_DH), const)
    tok = pl.BlockSpec((t, DA_DV), lambda i, h: (i, h))
    hbm = pl.BlockSpec(memory_space=pl.ANY)
    return pl.pallas_call(
        functools.partial(_da_sample_body, tk),
        out_shape=jax.ShapeDtypeStruct((b * t, DA_WIDTH), BF16),
        grid=(b, DA_HEADS),
        in_specs=[vec, vec, vec, vec, pl.BlockSpec((1, DA_DV), const), tok, tok, tok, hbm, hbm],
        out_specs=tok,
        scratch_shapes=[pltpu.VMEM((2, p, DA_DV), F32), pltpu.VMEM((2, p, DA_DV), F32),
                        pltpu.SemaphoreType.DMA((2, 2)),
                        pltpu.VMEM((2, t, 1), F32), pltpu.VMEM((2, t, 1), F32),
                        pltpu.VMEM((2, t, DA_DV), F32)],
        compiler_params=_params("arbitrary", "arbitrary"),
        name="diff_attn_sample",
    )(*lams, out_gain.reshape(1, DA_DV), q, k_new, v_new, cache_k, cache_v)


def _retention_body(lc, lg_ref, gain_ref, q_ref, k_ref, v_ref, g_ref, s0_ref, o_ref, s_out_ref,
                    state_ref):
    c = pl.program_id(2)

    @pl.when(c == 0)
    def _():
        state_ref[...] = s0_ref[0, 0]

    lg = lg_ref[0][:, :1]
    q = q_ref[...]
    k = k_ref[...]
    v = v_ref[...]
    rel = (lax.broadcasted_iota(jnp.int32, (lc, lc), 0)
           - lax.broadcasted_iota(jnp.int32, (lc, lc), 1)).astype(F32)
    decay = jnp.where(rel >= 0, jnp.exp(lg * jnp.maximum(rel, 0.0)), 0.0)
    scores = lax.dot_general(q, k, (((1,), (1,)), ((), ())), preferred_element_type=F32) * decay
    idx = lax.broadcasted_iota(jnp.int32, (lc, 1), 0).astype(F32)
    state = state_ref[...]
    o = jnp.dot(scores.astype(BF16), v, preferred_element_type=F32)
    o = o + jnp.dot(q, state.astype(BF16), preferred_element_type=F32) * jnp.exp(lg * (idx + 1.0))
    kw = k.astype(F32) * jnp.exp(lg * (lc - 1.0 - idx))
    s_new = state * jnp.exp(lg * lc) + jnp.dot(kw.T.astype(BF16), v, preferred_element_type=F32)
    state_ref[...] = s_new

    g = g_ref[...]
    o_ref[...] = (_rms(o) * gain_ref[...] * (g * _sigmoid(g))).astype(o_ref.dtype)

    @pl.when(c == pl.num_programs(2) - 1)
    def _():
        s_out_ref[0, 0] = s_new


def _retention(q, k, v, g, s0, out_gain, seq, lc):
    b = s0.shape[0]
    nc = seq // lc
    log_gamma = jnp.log(1.0 - 2.0 ** (-5.0 - jnp.arange(RET_HEADS, dtype=F32)))
    lg = jnp.broadcast_to(log_gamma.reshape(RET_HEADS, 1, 1), (RET_HEADS, 1, LANES))
    tok = pl.BlockSpec((lc, RET_DV), lambda bi, h, c: (bi * nc + c, h))
    st = pl.BlockSpec((1, 1, RET_DK, RET_DV), lambda bi, h, c: (bi, h, 0, 0))
    return pl.pallas_call(
        functools.partial(_retention_body, lc),
        out_shape=(jax.ShapeDtypeStruct((b * seq, RET_WIDTH), BF16),
                   jax.ShapeDtypeStruct(s0.shape, F32)),
        grid=(b, RET_HEADS, nc),
        in_specs=[pl.BlockSpec((1, 1, LANES), lambda bi, h, c: (h, 0, 0)),
                  pl.BlockSpec((1, RET_DV), lambda bi, h, c: (0, 0)),
                  tok, tok, tok, tok, st],
        out_specs=(tok, st),
        scratch_shapes=[pltpu.VMEM((RET_DK, RET_DV), F32)],
        compiler_params=_params("parallel", "parallel", "arbitrary"),
        name="retention",
    )(lg, out_gain.reshape(1, RET_DV), q, k, v, g, s0)


def _route(logits):
    lane = lax.broadcasted_iota(jnp.int32, logits.shape, 1)
    big = jnp.int32(LANES)
    neg = -jnp.inf
    gl = jnp.where((lane >= N_EXPERTS) & (lane < N_EXPERTS + N_GROUPS), logits, neg)
    g_max = jnp.max(gl, axis=-1, keepdims=True)
    g_idx = jnp.min(jnp.where(gl == g_max, lane - N_EXPERTS, big), axis=-1, keepdims=True)
    g_w = 1.0 / jnp.sum(jnp.exp(gl - g_max), axis=-1, keepdims=True)
    in_group = (lane < N_EXPERTS) & (lax.shift_right_logical(lane, 3) == g_idx)
    el = jnp.where(in_group, logits, neg)
    v1 = jnp.max(el, axis=-1, keepdims=True)
    i1 = jnp.min(jnp.where(el == v1, lane, big), axis=-1, keepdims=True)
    el2 = jnp.where(lane == i1, neg, el)
    v2 = jnp.max(el2, axis=-1, keepdims=True)
    i2 = jnp.min(jnp.where(el2 == v2, lane, big), axis=-1, keepdims=True)
    e2 = jnp.exp(v2 - v1)
    w1 = g_w / (1.0 + e2)
    w2 = g_w * e2 / (1.0 + e2)
    return jnp.where(lane == i1, w1, 0.0) + jnp.where(lane == i2, w2, 0.0)


def _outproj_body(x_ref, oda_ref, ort_ref, wo_ref, gf_ref, wr_hi_ref, wr_lo_ref, br_ref,
                  x1_ref, hf_ref, gate_ref):
    x1 = (x_ref[...]
          + jnp.dot(oda_ref[...], wo_ref[:DA_WIDTH, :], preferred_element_type=F32)
          + jnp.dot(ort_ref[...], wo_ref[DA_WIDTH:, :], preferred_element_type=F32))
    x1_ref[...] = x1
    hf = _rms(x1) * gf_ref[...]
    hf_hi = hf.astype(BF16)
    hf_ref[...] = hf_hi
    hf_lo = (hf - hf_hi.astype(F32)).astype(BF16)
    wr_hi = wr_hi_ref[...]
    logits = (jnp.dot(hf_hi, wr_hi, preferred_element_type=F32)
              + jnp.dot(hf_lo, wr_hi, preferred_element_type=F32)
              + jnp.dot(hf_hi, wr_lo_ref[...], preferred_element_type=F32)
              + br_ref[...])
    gate_ref[...] = _route(logits)


def _outproj(x, o_da, o_ret, w_out, ffn_g, wr_hi, wr_lo, br, tm):
    t, d = x.shape
    row = lambda i: (i, 0)
    const = lambda i: (0, 0)
    return pl.pallas_call(
        _outproj_body,
        out_shape=(jax.ShapeDtypeStruct((t, d), F32), jax.ShapeDtypeStruct((t, d), BF16),
                   jax.ShapeDtypeStruct((t, LANES), F32)),
        grid=(t // tm,),
        in_specs=[pl.BlockSpec((tm, d), row),
                  pl.BlockSpec((tm, DA_WIDTH), row),
                  pl.BlockSpec((tm, RET_WIDTH), row),
                  pl.BlockSpec((d, d), const),
                  pl.BlockSpec((1, d), const),
                  pl.BlockSpec((d, LANES), const),
                  pl.BlockSpec((d, LANES), const),
                  pl.BlockSpec((1, LANES), const)],
        out_specs=(pl.BlockSpec((tm, d), row), pl.BlockSpec((tm, d), row),
                   pl.BlockSpec((tm, LANES), row)),
        compiler_params=_params("parallel"),
        name="out_proj_router",
    )(x, o_da, o_ret, w_out, ffn_g.reshape(1, d), wr_hi, wr_lo, br)


def _moe_body(hf_ref, gate_ref, x1_ref, wg_ref, wu_ref, wd_ref, y_ref):
    e = pl.program_id(1)

    @pl.when(e == 0)
    def _():
        y_ref[...] = x1_ref[...]

    h = hf_ref[...]
    a = jnp.dot(h, wg_ref[0], preferred_element_type=F32)
    u = jnp.dot(h, wu_ref[0], preferred_element_type=F32)
    gate = gate_ref[...]
    lane = lax.broadcasted_iota(jnp.int32, gate.shape, 1)
    ge = jnp.sum(jnp.where(lane == e, gate, 0.0), axis=-1, keepdims=True)
    act = (a * _sigmoid(a)) * u * ge
    y_ref[...] += jnp.dot(act.astype(BF16), wd_ref[0], preferred_element_type=F32)


def _moe(hf, gate, x1, w_gate, w_up, w_down, tm):
    t, d = hf.shape
    row = lambda i, e: (i, 0)
    return pl.pallas_call(
        _moe_body,
        out_shape=jax.ShapeDtypeStruct((t, d), F32),
        grid=(t // tm, N_EXPERTS),
        in_specs=[pl.BlockSpec((tm, d), row),
                  pl.BlockSpec((tm, LANES), row),
                  pl.BlockSpec((tm, d), row),
                  pl.BlockSpec((1, d, D_FF), lambda i, e: (e, 0, 0)),
                  pl.BlockSpec((1, d, D_FF), lambda i, e: (e, 0, 0)),
                  pl.BlockSpec((1, D_FF, d), lambda i, e: (e, 0, 0))],
        out_specs=pl.BlockSpec((tm, d), row),
        compiler_params=_params("parallel", "arbitrary"),
        name="moe",
    )(hf, gate, x1, w_gate, w_up, w_down)


def _split_hi_lo(w):
    hi = w.astype(BF16)
    return hi, (w - hi.astype(F32)).astype(BF16)


def kernel(x_prompt, x_sample, cache_k_diff, cache_v_diff, state_retention, attn_norm_g, w_in, da_q_norm_g, da_k_norm_g, da_lambda_q1, da_lambda_k1, da_lambda_q2, da_lambda_k2, da_out_norm_g, ret_out_norm_g, w_out, ffn_norm_g, w_group, b_group, w_expert, b_expert, w_gate, w_up, w_down):
    assert w_in.shape[0] == 1, "single-layer model"
    bp, seq, d = x_prompt.shape
    bd, t_dec, _ = x_sample.shape
    past = cache_k_diff.shape[2]
    assert bp == 1

    w_in_b = w_in[0].astype(BF16)
    w_out_b = w_out[0].astype(BF16)
    w_gate_b = w_gate[0].astype(BF16)
    w_up_b = w_up[0].astype(BF16)
    w_down_b = w_down[0].astype(BF16)
    w_router = jnp.concatenate([w_expert[0].reshape(d, N_EXPERTS), w_group[0]], axis=1)
    w_router = jnp.pad(w_router, ((0, 0), (0, LANES - w_router.shape[1])))
    wr_hi, wr_lo = _split_hi_lo(w_router)
    b_router = jnp.concatenate([b_expert[0].reshape(N_EXPERTS), b_group[0]])
    b_router = jnp.pad(b_router, (0, LANES - b_router.shape[0])).reshape(1, LANES)

    lams = _lambda_args(da_lambda_q1[0], da_lambda_k1[0], da_lambda_q2[0], da_lambda_k2[0])

    def layer(x, cos, sin, attend, want_vt, ret_seq, ret_chunk, s0, tm):
        h = _rmsnorm(x, attn_norm_g[0], tm)
        q_da, k_da, k_da_b, v_da, v_da_b, v_da_t, q_r, k_r, v_r, g_r = _project(
            h, w_in_b, da_q_norm_g[0], da_k_norm_g[0], cos, sin, tm, want_vt)
        o_da = attend(q_da, k_da_b, v_da_b, v_da_t)
        o_ret, s_new = _retention(q_r, k_r, v_r, g_r, s0, ret_out_norm_g[0], ret_seq, ret_chunk)
        x1, hf, gate = _outproj(x, o_da, o_ret, w_out_b, ffn_norm_g[0], wr_hi, wr_lo, b_router,
                                min(tm, 256))
        y = _moe(hf, gate, x1, w_gate_b, w_up_b, w_down_b, tm)
        return y, k_da, v_da, s_new

    cos_p, sin_p = _rope_tables(seq, 0)
    y_p, k_p, v_p, s_p = layer(
        x_prompt.reshape(seq, d), cos_p, sin_p,
        lambda q, k, v, vt: _da_prompt(q, k, vt[0], lams, da_out_norm_g[0], 512),
        True, seq, 256, jnp.zeros((bp, RET_HEADS, RET_DK, RET_DV), F32), 512)

    cos_s, sin_s = _rope_tables(t_dec, past)
    cos_s = jnp.tile(cos_s, (bd, 1))
    sin_s = jnp.tile(sin_s, (bd, 1))
    y_s, k_s, v_s, s_s = layer(
        x_sample.reshape(bd * t_dec, d), cos_s, sin_s,
        lambda q, k, v, vt: _da_sample(q, k, v, cache_k_diff[0], cache_v_diff[0], lams,
                                       da_out_norm_g[0], t_dec, 1024),
        False, t_dec, t_dec, state_retention[0], bd * t_dec)

    return (y_p.reshape(bp, seq, d),
            y_s.reshape(bd, t_dec, d),
            k_p.reshape(1, bp, seq, DA_HEADS, 2 * DA_DH),
            v_p.reshape(1, bp, seq, DA_HEADS, DA_DV),
            s_p.reshape(1, bp, RET_HEADS, RET_DK, RET_DV),
            k_s.reshape(1, bd, t_dec, DA_HEADS, 2 * DA_DH),
            v_s.reshape(1, bd, t_dec, DA_HEADS, DA_DV),
            s_s.reshape(1, bd, RET_HEADS, RET_DK, RET_DV))
```

```python
import functools
import math

import jax
import jax.numpy as jnp
from jax import lax
from jax.experimental import pallas as pl
from jax.experimental.pallas import tpu as pltpu

D_MODEL = 2048
CHUNK = 64
DA_HEADS = 4
DA_DH = 128
DA_DV = 2 * DA_DH
DA_WIDTH = DA_HEADS * DA_DV
RET_HEADS = 4
RET_DK = 256
RET_DV = 256
RET_WIDTH = RET_HEADS * RET_DV
IN_GROUP = 1024
N_GROUPS = 4
EXP_PER_GROUP = 8
N_EXPERTS = N_GROUPS * EXP_PER_GROUP
D_FF = D_MODEL // 8
EPS = 1e-6
NEG_INF = -1e30
ROPE_BASE = 10000.0
LAM_INIT = 0.8 - 0.6 * math.exp(-0.3 * 0)

LANES = 128
VMEM_LIMIT = 48 * 1024 * 1024

F32 = jnp.float32
BF16 = jnp.bfloat16


def _params(*sem):
    return pltpu.CompilerParams(dimension_semantics=sem, vmem_limit_bytes=VMEM_LIMIT)


def _sigmoid(x):
    return 1.0 / (1.0 + jnp.exp(-x))


def _rms(x):
    return x * lax.rsqrt(jnp.mean(x * x, axis=-1, keepdims=True) + EPS)


def _rmsnorm_body(x_ref, g_ref, o_ref):
    o_ref[...] = (_rms(x_ref[...]) * g_ref[...]).astype(o_ref.dtype)


def _rmsnorm(x, g, tm):
    t, d = x.shape
    return pl.pallas_call(
        _rmsnorm_body,
        out_shape=jax.ShapeDtypeStruct((t, d), BF16),
        grid=(t // tm,),
        in_specs=[pl.BlockSpec((tm, d), lambda i: (i, 0)),
                  pl.BlockSpec((1, d), lambda i: (0, 0))],
        out_specs=pl.BlockSpec((tm, d), lambda i: (i, 0)),
        compiler_params=_params("parallel"),
        name="attn_norm",
    )(x, g.reshape(1, d))


def _rope_table_body(pos0, tr, invf_ref, cos_ref, sin_ref):
    row = lax.broadcasted_iota(jnp.int32, (tr, LANES), 0) + (pl.program_id(0) * tr + pos0)
    ang = row.astype(F32) * invf_ref[...]
    cos_ref[...] = jnp.cos(ang)
    sin_ref[...] = jnp.sin(ang)


def _rope_tables(n_pos, pos0):
    half = RET_DK // 2
    inv_freq = (ROPE_BASE ** (-jnp.arange(half, dtype=F32) / half)).reshape(1, half)
    tr = min(n_pos, 512)
    spec = pl.BlockSpec((tr, half), lambda i: (i, 0))
    return pl.pallas_call(
        functools.partial(_rope_table_body, pos0, tr),
        out_shape=(jax.ShapeDtypeStruct((n_pos, half), F32),) * 2,
        grid=(n_pos // tr,),
        in_specs=[pl.BlockSpec((1, half), lambda i: (0, 0))],
        out_specs=(spec, spec),
        compiler_params=_params("parallel"),
        name="rope_tables",
    )(inv_freq)


def _store_cols(o, sl, val):
    if len(o.shape) == 2:
        o[:, sl] = val.astype(o.dtype)
    else:
        hd, off = divmod(sl.start, DA_DV)
        o[:, hd, off:off + (sl.stop - sl.start)] = val.astype(o.dtype)


def _store_all(z, outs):
    for o in outs:
        for hd in range(DA_HEADS):
            sl = slice(hd * DA_DV, (hd + 1) * DA_DV)
            _store_cols(o, sl, z[:, sl])


def _proj_plain_body(h_ref, w_ref, *outs):
    z = jnp.dot(h_ref[...], w_ref[...], preferred_element_type=F32)
    _store_all(z, outs)


def _proj_plain_t_body(h_ref, w_ref, *outs):
    z = jnp.dot(h_ref[...], w_ref[...], preferred_element_type=F32)
    _store_all(z, outs[:-1])
    outs[-1][...] = z.T.astype(outs[-1].dtype)


def _proj_qknorm_body(scale, h_ref, w_ref, g_ref, *outs):
    z = jnp.dot(h_ref[...], w_ref[...], preferred_element_type=F32)
    g = g_ref[...]
    for c in range(IN_GROUP // DA_DH):
        sl = slice(c * DA_DH, (c + 1) * DA_DH)
        zc = _rms(z[:, sl]) * g
        for o in outs:
            _store_cols(o, sl, zc * scale if o.dtype == BF16 else zc)


def _proj_rotary_body(scale, h_ref, w_ref, cos_ref, sin_ref, o_ref):
    z = jnp.dot(h_ref[...], w_ref[...], preferred_element_type=F32) * scale
    cos = cos_ref[...]
    sin = sin_ref[...]
    half = RET_DK // 2
    for hd in range(RET_HEADS):
        x1 = z[:, hd * RET_DK: hd * RET_DK + half]
        x2 = z[:, hd * RET_DK + half: (hd + 1) * RET_DK]
        o_ref[:, hd * RET_DK: hd * RET_DK + half] = (x1 * cos - x2 * sin).astype(o_ref.dtype)
        o_ref[:, hd * RET_DK + half: (hd + 1) * RET_DK] = (x1 * sin + x2 * cos).astype(o_ref.dtype)


def _proj(h, w_in, group, body, extra, extra_specs, out_dtypes, tm, name, per_head=False,
          transposed_out=False):
    t, d = h.shape
    row = lambda i: (i, 0)
    split = lambda dt: per_head and dt == F32
    shape = lambda dt: (t, DA_HEADS, DA_DV) if split(dt) else (t, IN_GROUP)
    block = lambda dt: (pl.BlockSpec((tm, DA_HEADS, DA_DV), lambda i: (i, 0, 0)) if split(dt)
                        else pl.BlockSpec((tm, IN_GROUP), row))
    outs = tuple(jax.ShapeDtypeStruct(shape(dt), dt) for dt in out_dtypes)
    out_specs = tuple(block(dt) for dt in out_dtypes)
    if transposed_out:
        outs += (jax.ShapeDtypeStruct((IN_GROUP, t), BF16),)
        out_specs += (pl.BlockSpec((IN_GROUP, tm), lambda i: (0, i)),)
    res = pl.pallas_call(
        body,
        out_shape=outs,
        grid=(t // tm,),
        in_specs=[pl.BlockSpec((tm, d), row),
                  pl.BlockSpec((d, IN_GROUP), lambda i: (0, group))] + extra_specs,
        out_specs=out_specs,
        compiler_params=_params("parallel"),
        name=name,
    )(h, w_in, *extra)
    return res


def _project(h, w_in, q_norm_g, k_norm_g, cos, sin, tm, want_vt):
    half = RET_DK // 2
    row = lambda i: (i, 0)
    gspec = [pl.BlockSpec((1, DA_DH), lambda i: (0, 0))]
    rspec = [pl.BlockSpec((tm, half), row), pl.BlockSpec((tm, half), row)]
    (q_da,) = _proj(h, w_in, 0, functools.partial(_proj_qknorm_body, DA_DH ** -0.5),
                    [q_norm_g.reshape(1, DA_DH)], gspec, [BF16], tm, "proj_q_da")
    k_da, k_da_b = _proj(h, w_in, 1, functools.partial(_proj_qknorm_body, 1.0),
                         [k_norm_g.reshape(1, DA_DH)], gspec, [F32, BF16], tm, "proj_k_da",
                         per_head=True)
    v_da, v_da_b, *v_da_t = _proj(h, w_in, 2, _proj_plain_t_body if want_vt else _proj_plain_body,
                                  [], [], [F32, BF16], tm, "proj_v_da", per_head=True,
                                  transposed_out=want_vt)
    (q_r,) = _proj(h, w_in, 3, functools.partial(_proj_rotary_body, 1.0),
                   [cos, sin], rspec, [BF16], tm, "proj_q_ret")
    (k_r,) = _proj(h, w_in, 4, functools.partial(_proj_rotary_body, RET_DK ** -0.5),
                   [cos, sin], rspec, [BF16], tm, "proj_k_ret")
    (v_r,) = _proj(h, w_in, 5, _proj_plain_body, [], [], [BF16], tm, "proj_v_ret")
    (g_r,) = _proj(h, w_in, 6, _proj_plain_body, [], [], [F32], tm, "proj_g_ret")
    return q_da, k_da, k_da_b, v_da, v_da_b, v_da_t, q_r, k_r, v_r, g_r


def _diff_lambda(lq1, lk1, lq2, lk2):
    s1 = jnp.sum(lq1[...] * lk1[...], axis=-1, keepdims=True)
    s2 = jnp.sum(lq2[...] * lk2[...], axis=-1, keepdims=True)
    return jnp.exp(s1) - jnp.exp(s2) + LAM_INIT


def _softmax_step(q, k, v, m_ref, l_ref, acc_ref, idx, mask):
    s = lax.dot_general(q, k, (((1,), (1,)), ((), ())), preferred_element_type=F32)
    if mask is not None:
        s = jnp.where(mask, s, NEG_INF)
    m_prev = m_ref[idx]
    m_new = jnp.maximum(m_prev, jnp.max(s, axis=-1, keepdims=True))
    alpha = jnp.exp(m_prev - m_new)
    p = jnp.exp(s - m_new)
    l_ref[idx] = alpha * l_ref[idx] + jnp.sum(p, axis=-1, keepdims=True)
    acc_ref[idx] = alpha * acc_ref[idx] + jnp.dot(p.astype(BF16), v, preferred_element_type=F32)
    m_ref[idx] = m_new


def _softmax_init(m_ref, l_ref, acc_ref):
    m_ref[...] = jnp.full(m_ref.shape, NEG_INF, F32)
    l_ref[...] = jnp.zeros(l_ref.shape, F32)
    acc_ref[...] = jnp.zeros(acc_ref.shape, F32)


def _diff_combine(lam, gain, l_ref, acc_ref, i0, i1):
    o = acc_ref[i0] / l_ref[i0] - lam * (acc_ref[i1] / l_ref[i1])
    return _rms(o) * gain * (1.0 - LAM_INIT)


def _da_prompt_body(tq, lq1, lk1, lq2, lk2, gain_ref, q_ref, k_ref, vt_ref, o_ref,
                    m_ref, l_ref, acc_ref):
    qi = pl.program_id(1)
    _softmax_init(m_ref, l_ref, acc_ref)

    def block(start, mask):
        k = k_ref[pl.ds(start, tq), :]
        vt = vt_ref[:, pl.ds(start, tq)]
        sts = []
        for mp in range(2):
            sl = slice(mp * DA_DH, (mp + 1) * DA_DH)
            sts.append(lax.dot_general(k[:, sl], q_ref[:, sl], (((1,), (1,)), ((), ())),
                                       preferred_element_type=F32))
        pts, alphas = [], []
        for mp in range(2):
            st = sts[mp]
            if mask is not None:
                st = jnp.where(mask, st, NEG_INF)
            m_prev = m_ref[mp]
            m_new = jnp.maximum(m_prev, jnp.max(st, axis=0, keepdims=True))
            alpha = jnp.exp(m_prev - m_new)
            pt = jnp.exp(st - m_new)
            l_ref[mp] = alpha * l_ref[mp] + jnp.sum(pt, axis=0, keepdims=True)
            m_ref[mp] = m_new
            pts.append(pt.astype(BF16))
            alphas.append(alpha)
        for mp in range(2):
            acc_ref[mp] = alphas[mp] * acc_ref[mp] + jnp.dot(vt, pts[mp],
                                                             preferred_element_type=F32)

    def full_block(kb, carry):
        block(pl.multiple_of(kb * tq, tq), None)
        return carry

    lax.fori_loop(0, qi, full_block, 0)
    kc = lax.shift_right_logical(lax.broadcasted_iota(jnp.int32, (tq, tq), 0), 6)
    qc = lax.shift_right_logical(lax.broadcasted_iota(jnp.int32, (tq, tq), 1), 6)
    block(pl.multiple_of(qi * tq, tq), kc <= qc)

    lam = _diff_lambda(lq1, lk1, lq2, lk2)
    ot = acc_ref[0] / l_ref[0] - lam * (acc_ref[1] / l_ref[1])
    ot = ot * lax.rsqrt(jnp.mean(ot * ot, axis=0, keepdims=True) + EPS)
    ot = ot * gain_ref[...] * (1.0 - LAM_INIT)
    o_ref[...] = ot.T.astype(o_ref.dtype)


def _lambda_args(lq1, lk1, lq2, lk2):
    return [a.reshape(1, DA_DH) for a in (lq1, lk1, lq2, lk2)]


def _da_prompt(q, k, vt, lams, out_gain, tq):
    s = q.shape[0]
    assert CHUNK == 64 and tq % CHUNK == 0 and s % tq == 0
    const = lambda h, i: (0, 0)
    vec = pl.BlockSpec((1, DA_DH), const)
    return pl.pallas_call(
        functools.partial(_da_prompt_body, tq),
        out_shape=jax.ShapeDtypeStruct((s, DA_WIDTH), BF16),
        grid=(DA_HEADS, s // tq),
        in_specs=[vec, vec, vec, vec,
                  pl.BlockSpec((DA_DV, 1), const),
                  pl.BlockSpec((tq, DA_DV), lambda h, i: (i, h)),
                  pl.BlockSpec((s, DA_DV), lambda h, i: (0, h)),
                  pl.BlockSpec((DA_DV, s), lambda h, i: (h, 0))],
        out_specs=pl.BlockSpec((tq, DA_DV), lambda h, i: (i, h)),
        scratch_shapes=[pltpu.VMEM((2, 1, tq), F32), pltpu.VMEM((2, 1, tq), F32),
                        pltpu.VMEM((2, DA_DV, tq), F32)],
        compiler_params=_params("parallel", "parallel"),
        name="diff_attn_prompt",
    )(*lams, out_gain.reshape(DA_DV, 1), q, k, vt)


def _da_sample_body(tk, lq1, lk1, lq2, lk2, gain_ref, q_ref, kn_ref, vn_ref, ck_hbm, cv_hbm, o_ref,
                    kbuf, vbuf, sem, m_ref, l_ref, acc_ref):
    n = pl.program_id(0) * DA_HEADS + pl.program_id(1)
    total = pl.num_programs(0) * DA_HEADS
    slot = lax.rem(n, 2)

    def copies(step, to_slot):
        stream = lax.div(step, DA_HEADS)
        head = lax.rem(step, DA_HEADS)
        return (pltpu.make_async_copy(ck_hbm.at[stream, :, head, :], kbuf.at[to_slot], sem.at[0, to_slot]),
                pltpu.make_async_copy(cv_hbm.at[stream, :, head, :], vbuf.at[to_slot], sem.at[1, to_slot]))

    @pl.when(n == 0)
    def _():
        for c in copies(n, slot):
            c.start()

    @pl.when(n + 1 < total)
    def _():
        for c in copies(n + 1, 1 - slot):
            c.start()

    def attend(k, v):
        for mp in range(2):
            sl = slice(mp * DA_DH, (mp + 1) * DA_DH)
            _softmax_step(q_ref[:, sl], k[:, sl], v, m_ref, l_ref, acc_ref, mp, None)

    _softmax_init(m_ref, l_ref, acc_ref)
    attend(kn_ref[...], vn_ref[...])
    for c in copies(n, slot):
        c.wait()
    for j in range(kbuf.shape[1] // tk):
        attend(kbuf[slot, j * tk:(j + 1) * tk, :].astype(BF16),
               vbuf[slot, j * tk:(j + 1) * tk, :].astype(BF16))
    lam = _diff_lambda(lq1, lk1, lq2, lk2)
    o_ref[...] = _diff_combine(lam, gain_ref[...], l_ref, acc_ref, 0, 1).astype(o_ref.dtype)


def _da_sample(q, k_new, v_new, cache_k, cache_v, lams, out_gain, t, tk):
    b, p = cache_k.shape[:2]
    const = lambda i, h: (0, 0)
    vec = pl.BlockSpec((1, DA_DH), const)
    tok = pl.BlockSpec((t, DA_DV), lambda i, h: (i, h))
    hbm = pl.BlockSpec(memory_space=pl.ANY)
    return pl.pallas_call(
        functools.partial(_da_sample_body, tk),
        out_shape=jax.ShapeDtypeStruct((b * t, DA_WIDTH), BF16),
        grid=(b, DA_HEADS),
        in_specs=[vec, vec, vec, vec, pl.BlockSpec((1, DA_DV), const), tok, tok, tok, hbm, hbm],
        out_specs=tok,
        scratch_shapes=[pltpu.VMEM((2, p, DA_DV), F32), pltpu.VMEM((2, p, DA_DV), F32),
                        pltpu.SemaphoreType.DMA((2, 2)),
                        pltpu.VMEM((2, t, 1), F32), pltpu.VMEM((2, t, 1), F32),
                        pltpu.VMEM((2, t, DA_DV), F32)],
        compiler_params=_params("arbitrary", "arbitrary"),
        name="diff_attn_sample",
    )(*lams, out_gain.reshape(1, DA_DV), q, k_new, v_new, cache_k, cache_v)


def _retention_body(lc, lg_ref, gain_ref, q_ref, k_ref, v_ref, g_ref, s0_ref, o_ref, s_out_ref,
                    state_ref):
    c = pl.program_id(2)

    @pl.when(c == 0)
    def _():
        state_ref[...] = s0_ref[0, 0]

    lg = lg_ref[0][:, :1]
    q = q_ref[...]
    k = k_ref[...]
    v = v_ref[...]
    rel = (lax.broadcasted_iota(jnp.int32, (lc, lc), 0)
           - lax.broadcasted_iota(jnp.int32, (lc, lc), 1)).astype(F32)
    decay = jnp.where(rel >= 0, jnp.exp(lg * jnp.maximum(rel, 0.0)), 0.0)
    scores = lax.dot_general(q, k, (((1,), (1,)), ((), ())), preferred_element_type=F32) * decay
    idx = lax.broadcasted_iota(jnp.int32, (lc, 1), 0).astype(F32)
    state = state_ref[...]
    o = jnp.dot(scores.astype(BF16), v, preferred_element_type=F32)
    o = o + jnp.dot(q, state.astype(BF16), preferred_element_type=F32) * jnp.exp(lg * (idx + 1.0))
    kw = k.astype(F32) * jnp.exp(lg * (lc - 1.0 - idx))
    s_new = state * jnp.exp(lg * lc) + jnp.dot(kw.T.astype(BF16), v, preferred_element_type=F32)
    state_ref[...] = s_new

    g = g_ref[...]
    o_ref[...] = (_rms(o) * gain_ref[...] * (g * _sigmoid(g))).astype(o_ref.dtype)

    @pl.when(c == pl.num_programs(2) - 1)
    def _():
        s_out_ref[0, 0] = s_new


def _retention(q, k, v, g, s0, out_gain, seq, lc):
    b = s0.shape[0]
    nc = seq // lc
    log_gamma = jnp.log(1.0 - 2.0 ** (-5.0 - jnp.arange(RET_HEADS, dtype=F32)))
    lg = jnp.broadcast_to(log_gamma.reshape(RET_HEADS, 1, 1), (RET_HEADS, 1, LANES))
    tok = pl.BlockSpec((lc, RET_DV), lambda bi, h, c: (bi * nc + c, h))
    st = pl.BlockSpec((1, 1, RET_DK, RET_DV), lambda bi, h, c: (bi, h, 0, 0))
    return pl.pallas_call(
        functools.partial(_retention_body, lc),
        out_shape=(jax.ShapeDtypeStruct((b * seq, RET_WIDTH), BF16),
                   jax.ShapeDtypeStruct(s0.shape, F32)),
        grid=(b, RET_HEADS, nc),
        in_specs=[pl.BlockSpec((1, 1, LANES), lambda bi, h, c: (h, 0, 0)),
                  pl.BlockSpec((1, RET_DV), lambda bi, h, c: (0, 0)),
                  tok, tok, tok, tok, st],
        out_specs=(tok, st),
        scratch_shapes=[pltpu.VMEM((RET_DK, RET_DV), F32)],
        compiler_params=_params("parallel", "parallel", "arbitrary"),
        name="retention",
    )(lg, out_gain.reshape(1, RET_DV), q, k, v, g, s0)


def _route(logits):
    lane = lax.broadcasted_iota(jnp.int32, logits.shape, 1)
    big = jnp.int32(LANES)
    neg = -jnp.inf
    gl = jnp.where((lane >= N_EXPERTS) & (lane < N_EXPERTS + N_GROUPS), logits, neg)
    g_max = jnp.max(gl, axis=-1, keepdims=True)
    g_idx = jnp.min(jnp.where(gl == g_max, lane - N_EXPERTS, big), axis=-1, keepdims=True)
    g_w = 1.0 / jnp.sum(jnp.exp(gl - g_max), axis=-1, keepdims=True)
    in_group = (lane < N_EXPERTS) & (lax.shift_right_logical(lane, 3) == g_idx)
    el = jnp.where(in_group, logits, neg)
    v1 = jnp.max(el, axis=-1, keepdims=True)
    i1 = jnp.min(jnp.where(el == v1, lane, big), axis=-1, keepdims=True)
    el2 = jnp.where(lane == i1, neg, el)
    v2 = jnp.max(el2, axis=-1, keepdims=True)
    i2 = jnp.min(jnp.where(el2 == v2, lane, big), axis=-1, keepdims=True)
    e2 = jnp.exp(v2 - v1)
    w1 = g_w / (1.0 + e2)
    w2 = g_w * e2 / (1.0 + e2)
    return jnp.where(lane == i1, w1, 0.0) + jnp.where(lane == i2, w2, 0.0)


def _outproj_body(x_ref, oda_ref, ort_ref, wo_ref, gf_ref, wr_hi_ref, wr_lo_ref, br_ref,
                  x1_ref, hf_ref, gate_ref):
    x1 = (x_ref[...]
          + jnp.dot(oda_ref[...], wo_ref[:DA_WIDTH, :], preferred_element_type=F32)
          + jnp.dot(ort_ref[...], wo_ref[DA_WIDTH:, :], preferred_element_type=F32))
    x1_ref[...] = x1
    hf = _rms(x1) * gf_ref[...]
    hf_hi = hf.astype(BF16)
    hf_ref[...] = hf
    hf_lo = (hf - hf_hi.astype(F32)).astype(BF16)
    wr_hi = wr_hi_ref[...]
    logits = (jnp.dot(hf_hi, wr_hi, preferred_element_type=F32)
              + jnp.dot(hf_lo, wr_hi, preferred_element_type=F32)
              + jnp.dot(hf_hi, wr_lo_ref[...], preferred_element_type=F32)
              + br_ref[...])
    gate_ref[...] = _route(logits)


def _outproj(x, o_da, o_ret, w_out, ffn_g, wr_hi, wr_lo, br, tm):
    t, d = x.shape
    row = lambda i: (i, 0)
    const = lambda i: (0, 0)
    return pl.pallas_call(
        _outproj_body,
        out_shape=(jax.ShapeDtypeStruct((t, d), F32), jax.ShapeDtypeStruct((t, d), F32),
                   jax.ShapeDtypeStruct((t, LANES), F32)),
        grid=(t // tm,),
        in_specs=[pl.BlockSpec((tm, d), row),
                  pl.BlockSpec((tm, DA_WIDTH), row),
                  pl.BlockSpec((tm, RET_WIDTH), row),
                  pl.BlockSpec((d, d), const),
                  pl.BlockSpec((1, d), const),
                  pl.BlockSpec((d, LANES), const),
                  pl.BlockSpec((d, LANES), const),
                  pl.BlockSpec((1, LANES), const)],
        out_specs=(pl.BlockSpec((tm, d), row), pl.BlockSpec((tm, d), row),
                   pl.BlockSpec((tm, LANES), row)),
        compiler_params=_params("parallel"),
        name="out_proj_router",
    )(x, o_da, o_ret, w_out, ffn_g.reshape(1, d), wr_hi, wr_lo, br)


MOE_TILE = 256


def _row_copy(src_hbm, row, dst, dst_row, sem):
    return pltpu.make_async_copy(src_hbm.at[pl.ds(row, 1), :], dst.at[pl.ds(dst_row, 1), :], sem)


def _moe_expert_body(src_ref, texp_ref, nused_ref, hf_hbm, w_ref, wg_ref, wu_ref, wd_ref, ys_ref,
                     xbuf, sem):
    j = pl.program_id(0)
    slot = lax.rem(j, 2)
    nused = nused_ref[0]

    def gather(tile, to_slot, start):
        def one(r, carry):
            c = _row_copy(hf_hbm, src_ref[tile * MOE_TILE + r] if start else 0,
                          xbuf.at[to_slot], r, sem.at[to_slot])
            c.start() if start else c.wait()
            return carry
        lax.fori_loop(0, MOE_TILE, one, 0, unroll=8)

    @pl.when(j == 0)
    def _():
        gather(j, slot, True)

    @pl.when(j + 1 < nused)
    def _():
        gather(j + 1, 1 - slot, True)

    @pl.when(j < nused)
    def _():
        gather(j, slot, False)
        x = xbuf[slot].astype(BF16)
        a = jnp.dot(x, wg_ref[0], preferred_element_type=F32)
        u = jnp.dot(x, wu_ref[0], preferred_element_type=F32)
        act = (a * _sigmoid(a)) * u * w_ref[...]
        ys_ref[...] = jnp.dot(act.astype(BF16), wd_ref[0], preferred_element_type=F32)

    @pl.when(j >= nused)
    def _():
        ys_ref[...] = jnp.zeros(ys_ref.shape, F32)


def _moe_combine_body(pos_ref, x1_ref, ys_hbm, y_ref, gbuf, sem):
    i = pl.program_id(0)
    slot = lax.rem(i, 2)
    tc = x1_ref.shape[0]

    def gather(tile, to_slot, start):
        def one(r, carry):
            for k in range(2):
                c = _row_copy(ys_hbm, pos_ref[2 * (tile * tc + r) + k] if start else 0,
                              gbuf.at[to_slot, k], r, sem.at[to_slot])
                c.start() if start else c.wait()
            return carry
        lax.fori_loop(0, tc, one, 0, unroll=4)

    @pl.when(i == 0)
    def _():
        gather(i, slot, True)

    @pl.when(i + 1 < pl.num_programs(0))
    def _():
        gather(i + 1, 1 - slot, True)

    gather(i, slot, False)
    y_ref[...] = x1_ref[...] + gbuf[slot, 0] + gbuf[slot, 1]


def _moe(hf, gate, x1, w_gate, w_up, w_down, tm):
    t, d = hf.shape
    w_pair, e_pair = lax.top_k(gate[:, :N_EXPERTS], 2)
    e_flat = e_pair.reshape(-1).astype(jnp.int32)
    n_pair = 2 * t
    n_pad = n_pair + N_EXPERTS * MOE_TILE
    n_tiles = n_pad // MOE_TILE
    counts = jnp.sum(e_flat[:, None] == jnp.arange(N_EXPERTS, dtype=jnp.int32)[None, :], axis=0)
    padded = ((counts + MOE_TILE - 1) // MOE_TILE) * MOE_TILE
    ends = jnp.cumsum(padded)
    starts = ends - padded
    order = jnp.argsort(e_flat, stable=True)
    e_sorted = e_flat[order]
    rank = jnp.arange(n_pair, dtype=jnp.int32) - (jnp.cumsum(counts) - counts)[e_sorted]
    pos_sorted = (starts[e_sorted] + rank).astype(jnp.int32)
    pos = jnp.zeros((n_pair,), jnp.int32).at[order].set(pos_sorted)
    src_tok = jnp.zeros((n_pad,), jnp.int32).at[pos].set(jnp.arange(n_pair, dtype=jnp.int32) // 2)
    slot_w = jnp.zeros((n_pad,), F32).at[pos].set(w_pair.reshape(-1)).reshape(n_pad, 1)
    tile_start = jnp.arange(n_tiles, dtype=jnp.int32) * MOE_TILE
    tile_expert = jnp.minimum(jnp.searchsorted(ends, tile_start, side="right"),
                              N_EXPERTS - 1).astype(jnp.int32)
    n_used = (ends[-1] // MOE_TILE).astype(jnp.int32).reshape(1)

    wspec = lambda shape: pl.BlockSpec(shape, lambda j, src, te, nu: (te[j], 0, 0))
    ys = pl.pallas_call(
        _moe_expert_body,
        out_shape=jax.ShapeDtypeStruct((n_pad, d), F32),
        grid_spec=pltpu.PrefetchScalarGridSpec(
            num_scalar_prefetch=3,
            grid=(n_tiles,),
            in_specs=[pl.BlockSpec(memory_space=pl.ANY),
                      pl.BlockSpec((MOE_TILE, 1), lambda j, src, te, nu: (j, 0)),
                      wspec((1, d, D_FF)), wspec((1, d, D_FF)), wspec((1, D_FF, d))],
            out_specs=pl.BlockSpec((MOE_TILE, d), lambda j, src, te, nu: (j, 0)),
            scratch_shapes=[pltpu.VMEM((2, MOE_TILE, d), F32), pltpu.SemaphoreType.DMA((2,))]),
        compiler_params=_params("arbitrary"),
        name="moe_experts",
    )(src_tok, tile_expert, n_used, hf, slot_w, w_gate, w_up, w_down)

    tc = min(tm, MOE_TILE)
    return pl.pallas_call(
        _moe_combine_body,
        out_shape=jax.ShapeDtypeStruct((t, d), F32),
        grid_spec=pltpu.PrefetchScalarGridSpec(
            num_scalar_prefetch=1,
            grid=(t // tc,),
            in_specs=[pl.BlockSpec((tc, d), lambda i, p: (i, 0)),
                      pl.BlockSpec(memory_space=pl.ANY)],
            out_specs=pl.BlockSpec((tc, d), lambda i, p: (i, 0)),
            scratch_shapes=[pltpu.VMEM((2, 2, tc, d), F32), pltpu.SemaphoreType.DMA((2,))]),
        compiler_params=_params("arbitrary"),
        name="moe_combine",
    )(pos, x1, ys)


def _split_hi_lo(w):
    hi = w.astype(BF16)
    return hi, (w - hi.astype(F32)).astype(BF16)


def kernel(x_prompt, x_sample, cache_k_diff, cache_v_diff, state_retention, attn_norm_g, w_in, da_q_norm_g, da_k_norm_g, da_lambda_q1, da_lambda_k1, da_lambda_q2, da_lambda_k2, da_out_norm_g, ret_out_norm_g, w_out, ffn_norm_g, w_group, b_group, w_expert, b_expert, w_gate, w_up, w_down):
    assert w_in.shape[0] == 1, "single-layer model"
    bp, seq, d = x_prompt.shape
    bd, t_dec, _ = x_sample.shape
    past = cache_k_diff.shape[2]
    assert bp == 1

    w_in_b = w_in[0].astype(BF16)
    w_out_b = w_out[0].astype(BF16)
    w_gate_b = w_gate[0].astype(BF16)
    w_up_b = w_up[0].astype(BF16)
    w_down_b = w_down[0].astype(BF16)
    w_router = jnp.concatenate([w_expert[0].reshape(d, N_EXPERTS), w_group[0]], axis=1)
    w_router = jnp.pad(w_router, ((0, 0), (0, LANES - w_router.shape[1])))
    wr_hi, wr_lo = _split_hi_lo(w_router)
    b_router = jnp.concatenate([b_expert[0].reshape(N_EXPERTS), b_group[0]])
    b_router = jnp.pad(b_router, (0, LANES - b_router.shape[0])).reshape(1, LANES)

    lams = _lambda_args(da_lambda_q1[0], da_lambda_k1[0], da_lambda_q2[0], da_lambda_k2[0])

    def layer(x, cos, sin, attend, want_vt, ret_seq, ret_chunk, s0, tm):
        h = _rmsnorm(x, attn_norm_g[0], tm)
        q_da, k_da, k_da_b, v_da, v_da_b, v_da_t, q_r, k_r, v_r, g_r = _project(
            h, w_in_b, da_q_norm_g[0], da_k_norm_g[0], cos, sin, tm, want_vt)
        o_da = attend(q_da, k_da_b, v_da_b, v_da_t)
        o_ret, s_new = _retention(q_r, k_r, v_r, g_r, s0, ret_out_norm_g[0], ret_seq, ret_chunk)
        x1, hf, gate = _outproj(x, o_da, o_ret, w_out_b, ffn_norm_g[0], wr_hi, wr_lo, b_router,
                                min(tm, 256))
        y = _moe(hf, gate, x1, w_gate_b, w_up_b, w_down_b, tm)
        return y, k_da, v_da, s_new

    cos_p, sin_p = _rope_tables(seq, 0)
    y_p, k_p, v_p, s_p = layer(
        x_prompt.reshape(seq, d), cos_p, sin_p,
        lambda q, k, v, vt: _da_prompt(q, k, vt[0], lams, da_out_norm_g[0], 512),
        True, seq, 256, jnp.zeros((bp, RET_HEADS, RET_DK, RET_DV), F32), 512)

    cos_s, sin_s = _rope_tables(t_dec, past)
    cos_s = jnp.tile(cos_s, (bd, 1))
    sin_s = jnp.tile(sin_s, (bd, 1))
    y_s, k_s, v_s, s_s = layer(
        x_sample.reshape(bd * t_dec, d), cos_s, sin_s,
        lambda q, k, v, vt: _da_sample(q, k, v, cache_k_diff[0], cache_v_diff[0], lams,
                                       da_out_norm_g[0], t_dec, 1024),
        False, t_dec, t_dec, state_retention[0], bd * t_dec)

    return (y_p.reshape(bp, seq, d),
            y_s.reshape(bd, t_dec, d),
            k_p.reshape(1, bp, seq, DA_HEADS, 2 * DA_DH),
            v_p.reshape(1, bp, seq, DA_HEADS, DA_DV),
            s_p.reshape(1, bp, RET_HEADS, RET_DK, RET_DV),
            k_s.reshape(1, bd, t_dec, DA_HEADS, 2 * DA_DH),
            v_s.reshape(1, bd, t_dec, DA_HEADS, DA_DV),
            s_s.reshape(1, bd, RET_HEADS, RET_DK, RET_DV))
```

```python
import functools
import math

import jax
import jax.numpy as jnp
from jax import lax
from jax.experimental import pallas as pl
from jax.experimental.pallas import tpu as pltpu

D_MODEL = 2048
CHUNK = 64
DA_HEADS = 4
DA_DH = 128
DA_DV = 2 * DA_DH
DA_WIDTH = DA_HEADS * DA_DV
RET_HEADS = 4
RET_DK = 256
RET_DV = 256
RET_WIDTH = RET_HEADS * RET_DV
IN_GROUP = 1024
N_GROUPS = 4
EXP_PER_GROUP = 8
N_EXPERTS = N_GROUPS * EXP_PER_GROUP
D_FF = D_MODEL // 8
EPS = 1e-6
NEG_INF = -1e30
ROPE_BASE = 10000.0
LAM_INIT = 0.8 - 0.6 * math.exp(-0.3 * 0)

LANES = 128
VMEM_LIMIT = 48 * 1024 * 1024

F32 = jnp.float32
BF16 = jnp.bfloat16


def _params(*sem):
    return pltpu.CompilerParams(dimension_semantics=sem, vmem_limit_bytes=VMEM_LIMIT)


def _sigmoid(x):
    return 1.0 / (1.0 + jnp.exp(-x))


def _rms(x):
    return x * lax.rsqrt(jnp.mean(x * x, axis=-1, keepdims=True) + EPS)


def _rmsnorm_body(x_ref, g_ref, o_ref):
    o_ref[...] = (_rms(x_ref[...]) * g_ref[...]).astype(o_ref.dtype)


def _rmsnorm(x, g, tm):
    t, d = x.shape
    return pl.pallas_call(
        _rmsnorm_body,
        out_shape=jax.ShapeDtypeStruct((t, d), BF16),
        grid=(t // tm,),
        in_specs=[pl.BlockSpec((tm, d), lambda i: (i, 0)),
                  pl.BlockSpec((1, d), lambda i: (0, 0))],
        out_specs=pl.BlockSpec((tm, d), lambda i: (i, 0)),
        compiler_params=_params("parallel"),
        name="attn_norm",
    )(x, g.reshape(1, d))


def _rope_table_body(pos0, tr, invf_ref, cos_ref, sin_ref):
    row = lax.broadcasted_iota(jnp.int32, (tr, LANES), 0) + (pl.program_id(0) * tr + pos0)
    ang = row.astype(F32) * invf_ref[...]
    cos_ref[...] = jnp.cos(ang)
    sin_ref[...] = jnp.sin(ang)


def _rope_tables(n_pos, pos0):
    half = RET_DK // 2
    inv_freq = (ROPE_BASE ** (-jnp.arange(half, dtype=F32) / half)).reshape(1, half)
    tr = min(n_pos, 512)
    spec = pl.BlockSpec((tr, half), lambda i: (i, 0))
    return pl.pallas_call(
        functools.partial(_rope_table_body, pos0, tr),
        out_shape=(jax.ShapeDtypeStruct((n_pos, half), F32),) * 2,
        grid=(n_pos // tr,),
        in_specs=[pl.BlockSpec((1, half), lambda i: (0, 0))],
        out_specs=(spec, spec),
        compiler_params=_params("parallel"),
        name="rope_tables",
    )(inv_freq)


def _store_cols(o, sl, val):
    if len(o.shape) == 2:
        o[:, sl] = val.astype(o.dtype)
    else:
        hd, off = divmod(sl.start, DA_DV)
        o[:, hd, off:off + (sl.stop - sl.start)] = val.astype(o.dtype)


def _store_all(z, outs):
    for o in outs:
        for hd in range(DA_HEADS):
            sl = slice(hd * DA_DV, (hd + 1) * DA_DV)
            _store_cols(o, sl, z[:, sl])


def _proj_plain_body(h_ref, w_ref, *outs):
    z = jnp.dot(h_ref[...], w_ref[...], preferred_element_type=F32)
    _store_all(z, outs)


def _proj_plain_t_body(h_ref, w_ref, *outs):
    z = jnp.dot(h_ref[...], w_ref[...], preferred_element_type=F32)
    _store_all(z, outs[:-1])
    outs[-1][...] = z.T.astype(outs[-1].dtype)


def _proj_qknorm_body(scale, h_ref, w_ref, g_ref, *outs):
    z = jnp.dot(h_ref[...], w_ref[...], preferred_element_type=F32)
    g = g_ref[...]
    for c in range(IN_GROUP // DA_DH):
        sl = slice(c * DA_DH, (c + 1) * DA_DH)
        zc = _rms(z[:, sl]) * g
        for o in outs:
            _store_cols(o, sl, zc * scale if o.dtype == BF16 else zc)


def _proj_rotary_body(scale, h_ref, w_ref, cos_ref, sin_ref, o_ref):
    z = jnp.dot(h_ref[...], w_ref[...], preferred_element_type=F32) * scale
    cos = cos_ref[...]
    sin = sin_ref[...]
    half = RET_DK // 2
    for hd in range(RET_HEADS):
        x1 = z[:, hd * RET_DK: hd * RET_DK + half]
        x2 = z[:, hd * RET_DK + half: (hd + 1) * RET_DK]
        o_ref[:, hd * RET_DK: hd * RET_DK + half] = (x1 * cos - x2 * sin).astype(o_ref.dtype)
        o_ref[:, hd * RET_DK + half: (hd + 1) * RET_DK] = (x1 * sin + x2 * cos).astype(o_ref.dtype)


def _proj(h, w_in, group, body, extra, extra_specs, out_dtypes, tm, name, per_head=False,
          transposed_out=False):
    t, d = h.shape
    row = lambda i: (i, 0)
    split = lambda dt: per_head and dt == F32
    shape = lambda dt: (t, DA_HEADS, DA_DV) if split(dt) else (t, IN_GROUP)
    block = lambda dt: (pl.BlockSpec((tm, DA_HEADS, DA_DV), lambda i: (i, 0, 0)) if split(dt)
                        else pl.BlockSpec((tm, IN_GROUP), row))
    outs = tuple(jax.ShapeDtypeStruct(shape(dt), dt) for dt in out_dtypes)
    out_specs = tuple(block(dt) for dt in out_dtypes)
    if transposed_out:
        outs += (jax.ShapeDtypeStruct((IN_GROUP, t), BF16),)
        out_specs += (pl.BlockSpec((IN_GROUP, tm), lambda i: (0, i)),)
    res = pl.pallas_call(
        body,
        out_shape=outs,
        grid=(t // tm,),
        in_specs=[pl.BlockSpec((tm, d), row),
                  pl.BlockSpec((d, IN_GROUP), lambda i: (0, group))] + extra_specs,
        out_specs=out_specs,
        compiler_params=_params("parallel"),
        name=name,
    )(h, w_in, *extra)
    return res


def _project(h, w_in, q_norm_g, k_norm_g, cos, sin, tm, want_vt):
    half = RET_DK // 2
    row = lambda i: (i, 0)
    gspec = [pl.BlockSpec((1, DA_DH), lambda i: (0, 0))]
    rspec = [pl.BlockSpec((tm, half), row), pl.BlockSpec((tm, half), row)]
    (q_da,) = _proj(h, w_in, 0, functools.partial(_proj_qknorm_body, DA_DH ** -0.5),
                    [q_norm_g.reshape(1, DA_DH)], gspec, [BF16], tm, "proj_q_da")
    k_da, k_da_b = _proj(h, w_in, 1, functools.partial(_proj_qknorm_body, 1.0),
                         [k_norm_g.reshape(1, DA_DH)], gspec, [F32, BF16], tm, "proj_k_da",
                         per_head=True)
    v_da, v_da_b, *v_da_t = _proj(h, w_in, 2, _proj_plain_t_body if want_vt else _proj_plain_body,
                                  [], [], [F32, BF16], tm, "proj_v_da", per_head=True,
                                  transposed_out=want_vt)
    (q_r,) = _proj(h, w_in, 3, functools.partial(_proj_rotary_body, 1.0),
                   [cos, sin], rspec, [BF16], tm, "proj_q_ret")
    (k_r,) = _proj(h, w_in, 4, functools.partial(_proj_rotary_body, RET_DK ** -0.5),
                   [cos, sin], rspec, [BF16], tm, "proj_k_ret")
    (v_r,) = _proj(h, w_in, 5, _proj_plain_body, [], [], [BF16], tm, "proj_v_ret")
    (g_r,) = _proj(h, w_in, 6, _proj_plain_body, [], [], [F32], tm, "proj_g_ret")
    return q_da, k_da, k_da_b, v_da, v_da_b, v_da_t, q_r, k_r, v_r, g_r


def _diff_lambda(lq1, lk1, lq2, lk2):
    s1 = jnp.sum(lq1[...] * lk1[...], axis=-1, keepdims=True)
    s2 = jnp.sum(lq2[...] * lk2[...], axis=-1, keepdims=True)
    return jnp.exp(s1) - jnp.exp(s2) + LAM_INIT


def _softmax_step(q, k, v, m_ref, l_ref, acc_ref, idx, mask):
    s = lax.dot_general(q, k, (((1,), (1,)), ((), ())), preferred_element_type=F32)
    if mask is not None:
        s = jnp.where(mask, s, NEG_INF)
    m_prev = m_ref[idx]
    m_new = jnp.maximum(m_prev, jnp.max(s, axis=-1, keepdims=True))
    alpha = jnp.exp(m_prev - m_new)
    p = jnp.exp(s - m_new)
    l_ref[idx] = alpha * l_ref[idx] + jnp.sum(p, axis=-1, keepdims=True)
    acc_ref[idx] = alpha * acc_ref[idx] + jnp.dot(p.astype(BF16), v, preferred_element_type=F32)
    m_ref[idx] = m_new


def _softmax_init(m_ref, l_ref, acc_ref):
    m_ref[...] = jnp.full(m_ref.shape, NEG_INF, F32)
    l_ref[...] = jnp.zeros(l_ref.shape, F32)
    acc_ref[...] = jnp.zeros(acc_ref.shape, F32)


def _diff_combine(lam, gain, l_ref, acc_ref, i0, i1):
    o = acc_ref[i0] / l_ref[i0] - lam * (acc_ref[i1] / l_ref[i1])
    return _rms(o) * gain * (1.0 - LAM_INIT)


def _da_prompt_body(tq, lq1, lk1, lq2, lk2, gain_ref, q_ref, k_ref, vt_ref, o_ref,
                    m_ref, l_ref, acc_ref):
    qi = pl.program_id(1)
    _softmax_init(m_ref, l_ref, acc_ref)

    def block(start, mask):
        k = k_ref[pl.ds(start, tq), :]
        vt = vt_ref[:, pl.ds(start, tq)]
        sts = []
        for mp in range(2):
            sl = slice(mp * DA_DH, (mp + 1) * DA_DH)
            sts.append(lax.dot_general(k[:, sl], q_ref[:, sl], (((1,), (1,)), ((), ())),
                                       preferred_element_type=F32))
        pts, alphas = [], []
        for mp in range(2):
            st = sts[mp]
            if mask is not None:
                st = jnp.where(mask, st, NEG_INF)
            m_prev = m_ref[mp]
            m_new = jnp.maximum(m_prev, jnp.max(st, axis=0, keepdims=True))
            alpha = jnp.exp(m_prev - m_new)
            pt = jnp.exp(st - m_new)
            l_ref[mp] = alpha * l_ref[mp] + jnp.sum(pt, axis=0, keepdims=True)
            m_ref[mp] = m_new
            pts.append(pt.astype(BF16))
            alphas.append(alpha)
        for mp in range(2):
            acc_ref[mp] = alphas[mp] * acc_ref[mp] + jnp.dot(vt, pts[mp],
                                                             preferred_element_type=F32)

    def full_block(kb, carry):
        block(pl.multiple_of(kb * tq, tq), None)
        return carry

    lax.fori_loop(0, qi, full_block, 0)
    kc = lax.shift_right_logical(lax.broadcasted_iota(jnp.int32, (tq, tq), 0), 6)
    qc = lax.shift_right_logical(lax.broadcasted_iota(jnp.int32, (tq, tq), 1), 6)
    block(pl.multiple_of(qi * tq, tq), kc <= qc)

    lam = _diff_lambda(lq1, lk1, lq2, lk2)
    ot = acc_ref[0] / l_ref[0] - lam * (acc_ref[1] / l_ref[1])
    ot = ot * lax.rsqrt(jnp.mean(ot * ot, axis=0, keepdims=True) + EPS)
    ot = ot * gain_ref[...] * (1.0 - LAM_INIT)
    o_ref[...] = ot.T.astype(o_ref.dtype)


def _lambda_args(lq1, lk1, lq2, lk2):
    return [a.reshape(1, DA_DH) for a in (lq1, lk1, lq2, lk2)]


def _da_prompt(q, k, vt, lams, out_gain, tq):
    s = q.shape[0]
    assert CHUNK == 64 and tq % CHUNK == 0 and s % tq == 0
    const = lambda h, i: (0, 0)
    vec = pl.BlockSpec((1, DA_DH), const)
    return pl.pallas_call(
        functools.partial(_da_prompt_body, tq),
        out_shape=jax.ShapeDtypeStruct((s, DA_WIDTH), BF16),
        grid=(DA_HEADS, s // tq),
        in_specs=[vec, vec, vec, vec,
                  pl.BlockSpec((DA_DV, 1), const),
                  pl.BlockSpec((tq, DA_DV), lambda h, i: (i, h)),
                  pl.BlockSpec((s, DA_DV), lambda h, i: (0, h)),
                  pl.BlockSpec((DA_DV, s), lambda h, i: (h, 0))],
        out_specs=pl.BlockSpec((tq, DA_DV), lambda h, i: (i, h)),
        scratch_shapes=[pltpu.VMEM((2, 1, tq), F32), pltpu.VMEM((2, 1, tq), F32),
                        pltpu.VMEM((2, DA_DV, tq), F32)],
        compiler_params=_params("parallel", "parallel"),
        name="diff_attn_prompt",
    )(*lams, out_gain.reshape(DA_DV, 1), q, k, vt)


def _da_sample_body(tk, lq1, lk1, lq2, lk2, gain_ref, q_ref, kn_ref, vn_ref, ck_hbm, cv_hbm, o_ref,
                    kbuf, vbuf, sem, m_ref, l_ref, acc_ref):
    n = pl.program_id(0) * DA_HEADS + pl.program_id(1)
    total = pl.num_programs(0) * DA_HEADS
    slot = lax.rem(n, 2)

    def copies(step, to_slot):
        stream = lax.div(step, DA_HEADS)
        head = lax.rem(step, DA_HEADS)
        return (pltpu.make_async_copy(ck_hbm.at[stream, :, head, :], kbuf.at[to_slot], sem.at[0, to_slot]),
                pltpu.make_async_copy(cv_hbm.at[stream, :, head, :], vbuf.at[to_slot], sem.at[1, to_slot]))

    @pl.when(n == 0)
    def _():
        for c in copies(n, slot):
            c.start()

    @pl.when(n + 1 < total)
    def _():
        for c in copies(n + 1, 1 - slot):
            c.start()

    def attend(k, v):
        for mp in range(2):
            sl = slice(mp * DA_DH, (mp + 1) * DA_DH)
            _softmax_step(q_ref[:, sl], k[:, sl], v, m_ref, l_ref, acc_ref, mp, None)

    _softmax_init(m_ref, l_ref, acc_ref)
    attend(kn_ref[...], vn_ref[...])
    for c in copies(n, slot):
        c.wait()
    for j in range(kbuf.shape[1] // tk):
        attend(kbuf[slot, j * tk:(j + 1) * tk, :].astype(BF16),
               vbuf[slot, j * tk:(j + 1) * tk, :].astype(BF16))
    lam = _diff_lambda(lq1, lk1, lq2, lk2)
    o_ref[...] = _diff_combine(lam, gain_ref[...], l_ref, acc_ref, 0, 1).astype(o_ref.dtype)


def _da_sample(q, k_new, v_new, cache_k, cache_v, lams, out_gain, t, tk):
    b, p = cache_k.shape[:2]
    const = lambda i, h: (0, 0)
    vec = pl.BlockSpec((1, DA_DH), const)
    tok = pl.BlockSpec((t, DA_DV), lambda i, h: (i, h))
    hbm = pl.BlockSpec(memory_space=pl.ANY)
    return pl.pallas_call(
        functools.partial(_da_sample_body, tk),
        out_shape=jax.ShapeDtypeStruct((b * t, DA_WIDTH), BF16),
        grid=(b, DA_HEADS),
        in_specs=[vec, vec, vec, vec, pl.BlockSpec((1, DA_DV), const), tok, tok, tok, hbm, hbm],
        out_specs=tok,
        scratch_shapes=[pltpu.VMEM((2, p, DA_DV), F32), pltpu.VMEM((2, p, DA_DV), F32),
                        pltpu.SemaphoreType.DMA((2, 2)),
                        pltpu.VMEM((2, t, 1), F32), pltpu.VMEM((2, t, 1), F32),
                        pltpu.VMEM((2, t, DA_DV), F32)],
        compiler_params=_params("arbitrary", "arbitrary"),
        name="diff_attn_sample",
    )(*lams, out_gain.reshape(1, DA_DV), q, k_new, v_new, cache_k, cache_v)


def _retention_body(lc, lg_ref, gain_ref, q_ref, k_ref, v_ref, g_ref, s0_ref, o_ref, s_out_ref,
                    state_ref):
    c = pl.program_id(2)

    @pl.when(c == 0)
    def _():
        state_ref[...] = s0_ref[0, 0]

    lg = lg_ref[0][:, :1]
    q = q_ref[...]
    k = k_ref[...]
    v = v_ref[...]
    rel = (lax.broadcasted_iota(jnp.int32, (lc, lc), 0)
           - lax.broadcasted_iota(jnp.int32, (lc, lc), 1)).astype(F32)
    decay = jnp.where(rel >= 0, jnp.exp(lg * jnp.maximum(rel, 0.0)), 0.0)
    scores = lax.dot_general(q, k, (((1,), (1,)), ((), ())), preferred_element_type=F32) * decay
    idx = lax.broadcasted_iota(jnp.int32, (lc, 1), 0).astype(F32)
    state = state_ref[...]
    o = jnp.dot(scores.astype(BF16), v, preferred_element_type=F32)
    o = o + jnp.dot(q, state.astype(BF16), preferred_element_type=F32) * jnp.exp(lg * (idx + 1.0))
    kw = k.astype(F32) * jnp.exp(lg * (lc - 1.0 - idx))
    s_new = state * jnp.exp(lg * lc) + jnp.dot(kw.T.astype(BF16), v, preferred_element_type=F32)
    state_ref[...] = s_new

    g = g_ref[...]
    o_ref[...] = (_rms(o) * gain_ref[...] * (g * _sigmoid(g))).astype(o_ref.dtype)

    @pl.when(c == pl.num_programs(2) - 1)
    def _():
        s_out_ref[0, 0] = s_new


def _retention(q, k, v, g, s0, out_gain, seq, lc):
    b = s0.shape[0]
    nc = seq // lc
    log_gamma = jnp.log(1.0 - 2.0 ** (-5.0 - jnp.arange(RET_HEADS, dtype=F32)))
    lg = jnp.broadcast_to(log_gamma.reshape(RET_HEADS, 1, 1), (RET_HEADS, 1, LANES))
    tok = pl.BlockSpec((lc, RET_DV), lambda bi, h, c: (bi * nc + c, h))
    st = pl.BlockSpec((1, 1, RET_DK, RET_DV), lambda bi, h, c: (bi, h, 0, 0))
    return pl.pallas_call(
        functools.partial(_retention_body, lc),
        out_shape=(jax.ShapeDtypeStruct((b * seq, RET_WIDTH), BF16),
                   jax.ShapeDtypeStruct(s0.shape, F32)),
        grid=(b, RET_HEADS, nc),
        in_specs=[pl.BlockSpec((1, 1, LANES), lambda bi, h, c: (h, 0, 0)),
                  pl.BlockSpec((1, RET_DV), lambda bi, h, c: (0, 0)),
                  tok, tok, tok, tok, st],
        out_specs=(tok, st),
        scratch_shapes=[pltpu.VMEM((RET_DK, RET_DV), F32)],
        compiler_params=_params("parallel", "parallel", "arbitrary"),
        name="retention",
    )(lg, out_gain.reshape(1, RET_DV), q, k, v, g, s0)


def _route(logits):
    lane = lax.broadcasted_iota(jnp.int32, logits.shape, 1)
    big = jnp.int32(LANES)
    neg = -jnp.inf
    gl = jnp.where((lane >= N_EXPERTS) & (lane < N_EXPERTS + N_GROUPS), logits, neg)
    g_max = jnp.max(gl, axis=-1, keepdims=True)
    g_idx = jnp.min(jnp.where(gl == g_max, lane - N_EXPERTS, big), axis=-1, keepdims=True)
    g_w = 1.0 / jnp.sum(jnp.exp(gl - g_max), axis=-1, keepdims=True)
    in_group = (lane < N_EXPERTS) & (lax.shift_right_logical(lane, 3) == g_idx)
    el = jnp.where(in_group, logits, neg)
    v1 = jnp.max(el, axis=-1, keepdims=True)
    i1 = jnp.min(jnp.where(el == v1, lane, big), axis=-1, keepdims=True)
    el2 = jnp.where(lane == i1, neg, el)
    v2 = jnp.max(el2, axis=-1, keepdims=True)
    i2 = jnp.min(jnp.where(el2 == v2, lane, big), axis=-1, keepdims=True)
    e2 = jnp.exp(v2 - v1)
    w1 = g_w / (1.0 + e2)
    w2 = g_w * e2 / (1.0 + e2)
    return jnp.where(lane == i1, w1, 0.0) + jnp.where(lane == i2, w2, 0.0)


def _outproj_body(x_ref, oda_ref, ort_ref, wo_ref, gf_ref, wr_hi_ref, wr_lo_ref, br_ref,
                  x1_ref, hf_ref, gate_ref):
    x1 = (x_ref[...]
          + jnp.dot(oda_ref[...], wo_ref[:DA_WIDTH, :], preferred_element_type=F32)
          + jnp.dot(ort_ref[...], wo_ref[DA_WIDTH:, :], preferred_element_type=F32))
    x1_ref[...] = x1
    hf = _rms(x1) * gf_ref[...]
    hf_hi = hf.astype(BF16)
    hf_ref[...] = hf
    hf_lo = (hf - hf_hi.astype(F32)).astype(BF16)
    wr_hi = wr_hi_ref[...]
    logits = (jnp.dot(hf_hi, wr_hi, preferred_element_type=F32)
              + jnp.dot(hf_lo, wr_hi, preferred_element_type=F32)
              + jnp.dot(hf_hi, wr_lo_ref[...], preferred_element_type=F32)
              + br_ref[...])
    gate_ref[...] = _route(logits)


def _outproj(x, o_da, o_ret, w_out, ffn_g, wr_hi, wr_lo, br, tm):
    t, d = x.shape
    row = lambda i: (i, 0)
    const = lambda i: (0, 0)
    return pl.pallas_call(
        _outproj_body,
        out_shape=(jax.ShapeDtypeStruct((t, d), F32), jax.ShapeDtypeStruct((t, d), F32),
                   jax.ShapeDtypeStruct((t, LANES), F32)),
        grid=(t // tm,),
        in_specs=[pl.BlockSpec((tm, d), row),
                  pl.BlockSpec((tm, DA_WIDTH), row),
                  pl.BlockSpec((tm, RET_WIDTH), row),
                  pl.BlockSpec((d, d), const),
                  pl.BlockSpec((1, d), const),
                  pl.BlockSpec((d, LANES), const),
                  pl.BlockSpec((d, LANES), const),
                  pl.BlockSpec((1, LANES), const)],
        out_specs=(pl.BlockSpec((tm, d), row), pl.BlockSpec((tm, d), row),
                   pl.BlockSpec((tm, LANES), row)),
        compiler_params=_params("parallel"),
        name="out_proj_router",
    )(x, o_da, o_ret, w_out, ffn_g.reshape(1, d), wr_hi, wr_lo, br)


MOE_TILE = 256


def _row_copy(src_hbm, row, dst, dst_row, sem):
    return pltpu.make_async_copy(src_hbm.at[pl.ds(row, 1), :], dst.at[pl.ds(dst_row, 1), :], sem)


def _moe_expert_body(src_ref, texp_ref, nused_ref, hf_hbm, wg_ref, wu_ref, wd_ref, ys_ref,
                     xbuf, sem):
    j = pl.program_id(0)
    slot = lax.rem(j, 2)
    nused = nused_ref[0]

    def gather(tile, to_slot, start):
        def one(r, carry):
            c = _row_copy(hf_hbm, src_ref[tile * MOE_TILE + r] if start else 0,
                          xbuf.at[to_slot], r, sem.at[to_slot])
            c.start() if start else c.wait()
            return carry
        lax.fori_loop(0, MOE_TILE, one, 0, unroll=8)

    @pl.when(j == 0)
    def _():
        gather(j, slot, True)

    @pl.when(j + 1 < nused)
    def _():
        gather(j + 1, 1 - slot, True)

    @pl.when(j < nused)
    def _():
        gather(j, slot, False)
        x = xbuf[slot].astype(BF16)
        a = jnp.dot(x, wg_ref[0], preferred_element_type=F32)
        u = jnp.dot(x, wu_ref[0], preferred_element_type=F32)
        act = (a * _sigmoid(a)) * u
        ys_ref[...] = jnp.dot(act.astype(BF16), wd_ref[0], preferred_element_type=F32)

    @pl.when(j >= nused)
    def _():
        ys_ref[...] = jnp.zeros(ys_ref.shape, F32)


def _moe_combine_body(pos_ref, x1_ref, w_ref, ys_hbm, y_ref, gbuf, sem):
    i = pl.program_id(0)
    slot = lax.rem(i, 2)
    tc = x1_ref.shape[0]

    def gather(tile, to_slot, start):
        def one(r, carry):
            for k in range(2):
                c = _row_copy(ys_hbm, pos_ref[2 * (tile * tc + r) + k] if start else 0,
                              gbuf.at[to_slot, k], r, sem.at[to_slot])
                c.start() if start else c.wait()
            return carry
        lax.fori_loop(0, tc, one, 0, unroll=4)

    @pl.when(i == 0)
    def _():
        gather(i, slot, True)

    @pl.when(i + 1 < pl.num_programs(0))
    def _():
        gather(i + 1, 1 - slot, True)

    gather(i, slot, False)
    w = w_ref[...]
    y_ref[...] = x1_ref[...] + w[:, 0:1] * gbuf[slot, 0] + w[:, 1:2] * gbuf[slot, 1]


def _moe_dense_body(hf_ref, gate_ref, x1_ref, wg_ref, wu_ref, wd_ref, y_ref):
    e = pl.program_id(1)

    @pl.when(e == 0)
    def _():
        y_ref[...] = x1_ref[...]

    h = hf_ref[...].astype(BF16)
    a = jnp.dot(h, wg_ref[0], preferred_element_type=F32)
    u = jnp.dot(h, wu_ref[0], preferred_element_type=F32)
    gate = gate_ref[...]
    lane = lax.broadcasted_iota(jnp.int32, gate.shape, 1)
    ge = jnp.sum(jnp.where(lane == e, gate, 0.0), axis=-1, keepdims=True)
    act = (a * _sigmoid(a)) * u * ge
    y_ref[...] += jnp.dot(act.astype(BF16), wd_ref[0], preferred_element_type=F32)


def _moe_dense(hf, gate, x1, w_gate, w_up, w_down, tm):
    t, d = hf.shape
    row = lambda i, e: (i, 0)
    return pl.pallas_call(
        _moe_dense_body,
        out_shape=jax.ShapeDtypeStruct((t, d), F32),
        grid=(t // tm, N_EXPERTS),
        in_specs=[pl.BlockSpec((tm, d), row),
                  pl.BlockSpec((tm, LANES), row),
                  pl.BlockSpec((tm, d), row),
                  pl.BlockSpec((1, d, D_FF), lambda i, e: (e, 0, 0)),
                  pl.BlockSpec((1, d, D_FF), lambda i, e: (e, 0, 0)),
                  pl.BlockSpec((1, D_FF, d), lambda i, e: (e, 0, 0))],
        out_specs=pl.BlockSpec((tm, d), row),
        compiler_params=_params("parallel", "arbitrary"),
        name="moe_dense",
    )(hf, gate, x1, w_gate, w_up, w_down)


def _moe(hf, gate, x1, w_gate, w_up, w_down, tm):
    t, d = hf.shape
    n_pair = 2 * t
    if n_pair < N_EXPERTS * MOE_TILE:
        return _moe_dense(hf, gate, x1, w_gate, w_up, w_down, tm)
    w_pair, e_pair = lax.top_k(gate[:, :N_EXPERTS], 2)
    e_flat = e_pair.reshape(-1).astype(jnp.int32)
    n_pad = n_pair + N_EXPERTS * MOE_TILE
    n_tiles = n_pad // MOE_TILE
    onehot = (e_flat[:, None] == jnp.arange(N_EXPERTS, dtype=jnp.int32)[None, :]).astype(jnp.int32)
    csum = jnp.cumsum(onehot, axis=0)
    counts = csum[-1]
    rank = jnp.take_along_axis(csum, e_flat[:, None], axis=1)[:, 0] - 1
    padded = ((counts + MOE_TILE - 1) // MOE_TILE) * MOE_TILE
    ends = jnp.cumsum(padded)
    starts = ends - padded
    pos = (starts[e_flat] + rank).astype(jnp.int32)
    tile_start = jnp.arange(n_tiles, dtype=jnp.int32) * MOE_TILE
    tile_expert = jnp.minimum(jnp.searchsorted(ends, tile_start, side="right"),
                              N_EXPERTS - 1).astype(jnp.int32)
    n_used = (ends[-1] // MOE_TILE).astype(jnp.int32).reshape(1)
    order = jnp.argsort(e_flat, stable=True).astype(jnp.int32)
    slot_e = jnp.repeat(tile_expert, MOE_TILE)
    slot_r = jnp.arange(n_pad, dtype=jnp.int32) - starts[slot_e]
    sorted_idx = jnp.clip((jnp.cumsum(counts) - counts)[slot_e] + slot_r, 0, n_pair - 1)
    src_tok = jnp.where(slot_r < counts[slot_e], order[sorted_idx] // 2, 0).astype(jnp.int32)

    wspec = lambda shape: pl.BlockSpec(shape, lambda j, src, te, nu: (te[j], 0, 0))
    ys = pl.pallas_call(
        _moe_expert_body,
        out_shape=jax.ShapeDtypeStruct((n_pad, d), F32),
        grid_spec=pltpu.PrefetchScalarGridSpec(
            num_scalar_prefetch=3,
            grid=(n_tiles,),
            in_specs=[pl.BlockSpec(memory_space=pl.ANY),
                      wspec((1, d, D_FF)), wspec((1, d, D_FF)), wspec((1, D_FF, d))],
            out_specs=pl.BlockSpec((MOE_TILE, d), lambda j, src, te, nu: (j, 0)),
            scratch_shapes=[pltpu.VMEM((2, MOE_TILE, d), F32), pltpu.SemaphoreType.DMA((2,))]),
        compiler_params=_params("arbitrary"),
        name="moe_experts",
    )(src_tok, tile_expert, n_used, hf, w_gate, w_up, w_down)

    tc = min(tm, MOE_TILE)
    return pl.pallas_call(
        _moe_combine_body,
        out_shape=jax.ShapeDtypeStruct((t, d), F32),
        grid_spec=pltpu.PrefetchScalarGridSpec(
            num_scalar_prefetch=1,
            grid=(t // tc,),
            in_specs=[pl.BlockSpec((tc, d), lambda i, p: (i, 0)),
                      pl.BlockSpec((tc, 2), lambda i, p: (i, 0)),
                      pl.BlockSpec(memory_space=pl.ANY)],
            out_specs=pl.BlockSpec((tc, d), lambda i, p: (i, 0)),
            scratch_shapes=[pltpu.VMEM((2, 2, tc, d), F32), pltpu.SemaphoreType.DMA((2,))]),
        compiler_params=_params("arbitrary"),
        name="moe_combine",
    )(pos, x1, w_pair, ys)


def _split_hi_lo(w):
    hi = w.astype(BF16)
    return hi, (w - hi.astype(F32)).astype(BF16)


def kernel(x_prompt, x_sample, cache_k_diff, cache_v_diff, state_retention, attn_norm_g, w_in, da_q_norm_g, da_k_norm_g, da_lambda_q1, da_lambda_k1, da_lambda_q2, da_lambda_k2, da_out_norm_g, ret_out_norm_g, w_out, ffn_norm_g, w_group, b_group, w_expert, b_expert, w_gate, w_up, w_down):
    assert w_in.shape[0] == 1, "single-layer model"
    bp, seq, d = x_prompt.shape
    bd, t_dec, _ = x_sample.shape
    past = cache_k_diff.shape[2]
    assert bp == 1

    w_in_b = w_in[0].astype(BF16)
    w_out_b = w_out[0].astype(BF16)
    w_gate_b = w_gate[0].astype(BF16)
    w_up_b = w_up[0].astype(BF16)
    w_down_b = w_down[0].astype(BF16)
    w_router = jnp.concatenate([w_expert[0].reshape(d, N_EXPERTS), w_group[0]], axis=1)
    w_router = jnp.pad(w_router, ((0, 0), (0, LANES - w_router.shape[1])))
    wr_hi, wr_lo = _split_hi_lo(w_router)
    b_router = jnp.concatenate([b_expert[0].reshape(N_EXPERTS), b_group[0]])
    b_router = jnp.pad(b_router, (0, LANES - b_router.shape[0])).reshape(1, LANES)

    lams = _lambda_args(da_lambda_q1[0], da_lambda_k1[0], da_lambda_q2[0], da_lambda_k2[0])

    def layer(x, cos, sin, attend, want_vt, ret_seq, ret_chunk, s0, tm):
        h = _rmsnorm(x, attn_norm_g[0], tm)
        q_da, k_da, k_da_b, v_da, v_da_b, v_da_t, q_r, k_r, v_r, g_r = _project(
            h, w_in_b, da_q_norm_g[0], da_k_norm_g[0], cos, sin, tm, want_vt)
        o_da = attend(q_da, k_da_b, v_da_b, v_da_t)
        o_ret, s_new = _retention(q_r, k_r, v_r, g_r, s0, ret_out_norm_g[0], ret_seq, ret_chunk)
        x1, hf, gate = _outproj(x, o_da, o_ret, w_out_b, ffn_norm_g[0], wr_hi, wr_lo, b_router,
                                min(tm, 256))
        y = _moe(hf, gate, x1, w_gate_b, w_up_b, w_down_b, tm)
        return y, k_da, v_da, s_new

    cos_p, sin_p = _rope_tables(seq, 0)
    y_p, k_p, v_p, s_p = layer(
        x_prompt.reshape(seq, d), cos_p, sin_p,
        lambda q, k, v, vt: _da_prompt(q, k, vt[0], lams, da_out_norm_g[0], 512),
        True, seq, 256, jnp.zeros((bp, RET_HEADS, RET_DK, RET_DV), F32), 512)

    cos_s, sin_s = _rope_tables(t_dec, past)
    cos_s = jnp.tile(cos_s, (bd, 1))
    sin_s = jnp.tile(sin_s, (bd, 1))
    y_s, k_s, v_s, s_s = layer(
        x_sample.reshape(bd * t_dec, d), cos_s, sin_s,
        lambda q, k, v, vt: _da_sample(q, k, v, cache_k_diff[0], cache_v_diff[0], lams,
                                       da_out_norm_g[0], t_dec, 1024),
        False, t_dec, t_dec, state_retention[0], bd * t_dec)

    return (y_p.reshape(bp, seq, d),
            y_s.reshape(bd, t_dec, d),
            k_p.reshape(1, bp, seq, DA_HEADS, 2 * DA_DH),
            v_p.reshape(1, bp, seq, DA_HEADS, DA_DV),
            s_p.reshape(1, bp, RET_HEADS, RET_DK, RET_DV),
            k_s.reshape(1, bd, t_dec, DA_HEADS, 2 * DA_DH),
            v_s.reshape(1, bd, t_dec, DA_HEADS, DA_DV),
            s_s.reshape(1, bd, RET_HEADS, RET_DK, RET_DV))
```

```python
import functools
import math

import jax
import jax.numpy as jnp
from jax import lax
from jax.experimental import pallas as pl
from jax.experimental.pallas import tpu as pltpu

D_MODEL = 2048
CHUNK = 64
DA_HEADS = 4
DA_DH = 128
DA_DV = 2 * DA_DH
DA_WIDTH = DA_HEADS * DA_DV
RET_HEADS = 4
RET_DK = 256
RET_DV = 256
RET_WIDTH = RET_HEADS * RET_DV
IN_GROUP = 1024
N_GROUPS = 4
EXP_PER_GROUP = 8
N_EXPERTS = N_GROUPS * EXP_PER_GROUP
D_FF = D_MODEL // 8
EPS = 1e-6
NEG_INF = -1e30
ROPE_BASE = 10000.0
LAM_INIT = 0.8 - 0.6 * math.exp(-0.3 * 0)

LANES = 128
VMEM_LIMIT = 48 * 1024 * 1024

F32 = jnp.float32
BF16 = jnp.bfloat16


def _params(*sem):
    return pltpu.CompilerParams(dimension_semantics=sem, vmem_limit_bytes=VMEM_LIMIT)


def _sigmoid(x):
    return 1.0 / (1.0 + jnp.exp(-x))


def _rms(x):
    return x * lax.rsqrt(jnp.mean(x * x, axis=-1, keepdims=True) + EPS)


def _rmsnorm_body(x_ref, g_ref, o_ref):
    o_ref[...] = (_rms(x_ref[...]) * g_ref[...]).astype(o_ref.dtype)


def _rmsnorm(x, g, tm):
    t, d = x.shape
    return pl.pallas_call(
        _rmsnorm_body,
        out_shape=jax.ShapeDtypeStruct((t, d), BF16),
        grid=(t // tm,),
        in_specs=[pl.BlockSpec((tm, d), lambda i: (i, 0)),
                  pl.BlockSpec((1, d), lambda i: (0, 0))],
        out_specs=pl.BlockSpec((tm, d), lambda i: (i, 0)),
        compiler_params=_params("parallel"),
        name="attn_norm",
    )(x, g.reshape(1, d))


def _rope_table_body(pos0, tr, invf_ref, cos_ref, sin_ref):
    row = lax.broadcasted_iota(jnp.int32, (tr, LANES), 0) + (pl.program_id(0) * tr + pos0)
    ang = row.astype(F32) * invf_ref[...]
    cos_ref[...] = jnp.cos(ang)
    sin_ref[...] = jnp.sin(ang)


def _rope_tables(n_pos, pos0):
    half = RET_DK // 2
    inv_freq = (ROPE_BASE ** (-jnp.arange(half, dtype=F32) / half)).reshape(1, half)
    tr = min(n_pos, 512)
    spec = pl.BlockSpec((tr, half), lambda i: (i, 0))
    return pl.pallas_call(
        functools.partial(_rope_table_body, pos0, tr),
        out_shape=(jax.ShapeDtypeStruct((n_pos, half), F32),) * 2,
        grid=(n_pos // tr,),
        in_specs=[pl.BlockSpec((1, half), lambda i: (0, 0))],
        out_specs=(spec, spec),
        compiler_params=_params("parallel"),
        name="rope_tables",
    )(inv_freq)


def _store_cols(o, sl, val):
    if len(o.shape) == 2:
        o[:, sl] = val.astype(o.dtype)
    else:
        hd, off = divmod(sl.start, DA_DV)
        o[:, hd, off:off + (sl.stop - sl.start)] = val.astype(o.dtype)


def _store_all(z, outs):
    for o in outs:
        for hd in range(DA_HEADS):
            sl = slice(hd * DA_DV, (hd + 1) * DA_DV)
            _store_cols(o, sl, z[:, sl])


def _proj_plain_body(h_ref, w_ref, *outs):
    z = jnp.dot(h_ref[...], w_ref[...], preferred_element_type=F32)
    _store_all(z, outs)


def _proj_plain_t_body(h_ref, w_ref, *outs):
    z = jnp.dot(h_ref[...], w_ref[...], preferred_element_type=F32)
    _store_all(z, outs[:-1])
    outs[-1][...] = z.T.astype(outs[-1].dtype)


def _proj_qknorm_body(scale, h_ref, w_ref, g_ref, *outs):
    z = jnp.dot(h_ref[...], w_ref[...], preferred_element_type=F32)
    g = g_ref[...]
    for c in range(IN_GROUP // DA_DH):
        sl = slice(c * DA_DH, (c + 1) * DA_DH)
        zc = _rms(z[:, sl]) * g
        for o in outs:
            _store_cols(o, sl, zc * scale if o.dtype == BF16 else zc)


def _proj_rotary_body(scale, h_ref, w_ref, cos_ref, sin_ref, o_ref):
    z = jnp.dot(h_ref[...], w_ref[...], preferred_element_type=F32) * scale
    cos = cos_ref[...]
    sin = sin_ref[...]
    half = RET_DK // 2
    for hd in range(RET_HEADS):
        x1 = z[:, hd * RET_DK: hd * RET_DK + half]
        x2 = z[:, hd * RET_DK + half: (hd + 1) * RET_DK]
        o_ref[:, hd * RET_DK: hd * RET_DK + half] = (x1 * cos - x2 * sin).astype(o_ref.dtype)
        o_ref[:, hd * RET_DK + half: (hd + 1) * RET_DK] = (x1 * sin + x2 * cos).astype(o_ref.dtype)


def _proj(h, w_in, group, body, extra, extra_specs, out_dtypes, tm, name, per_head=False,
          transposed_out=False):
    t, d = h.shape
    row = lambda i: (i, 0)
    split = lambda dt: per_head and dt == F32
    shape = lambda dt: (t, DA_HEADS, DA_DV) if split(dt) else (t, IN_GROUP)
    block = lambda dt: (pl.BlockSpec((tm, DA_HEADS, DA_DV), lambda i: (i, 0, 0)) if split(dt)
                        else pl.BlockSpec((tm, IN_GROUP), row))
    outs = tuple(jax.ShapeDtypeStruct(shape(dt), dt) for dt in out_dtypes)
    out_specs = tuple(block(dt) for dt in out_dtypes)
    if transposed_out:
        outs += (jax.ShapeDtypeStruct((IN_GROUP, t), BF16),)
        out_specs += (pl.BlockSpec((IN_GROUP, tm), lambda i: (0, i)),)
    res = pl.pallas_call(
        body,
        out_shape=outs,
        grid=(t // tm,),
        in_specs=[pl.BlockSpec((tm, d), row),
                  pl.BlockSpec((d, IN_GROUP), lambda i: (0, group))] + extra_specs,
        out_specs=out_specs,
        compiler_params=_params("parallel"),
        name=name,
    )(h, w_in, *extra)
    return res


def _project(h, w_in, q_norm_g, k_norm_g, cos, sin, tm, want_vt):
    half = RET_DK // 2
    row = lambda i: (i, 0)
    gspec = [pl.BlockSpec((1, DA_DH), lambda i: (0, 0))]
    rspec = [pl.BlockSpec((tm, half), row), pl.BlockSpec((tm, half), row)]
    (q_da,) = _proj(h, w_in, 0, functools.partial(_proj_qknorm_body, DA_DH ** -0.5),
                    [q_norm_g.reshape(1, DA_DH)], gspec, [BF16], tm, "proj_q_da")
    k_da, k_da_b = _proj(h, w_in, 1, functools.partial(_proj_qknorm_body, 1.0),
                         [k_norm_g.reshape(1, DA_DH)], gspec, [F32, BF16], tm, "proj_k_da",
                         per_head=True)
    v_da, v_da_b, *v_da_t = _proj(h, w_in, 2, _proj_plain_t_body if want_vt else _proj_plain_body,
                                  [], [], [F32, BF16], tm, "proj_v_da", per_head=True,
                                  transposed_out=want_vt)
    (q_r,) = _proj(h, w_in, 3, functools.partial(_proj_rotary_body, 1.0),
                   [cos, sin], rspec, [BF16], tm, "proj_q_ret")
    (k_r,) = _proj(h, w_in, 4, functools.partial(_proj_rotary_body, RET_DK ** -0.5),
                   [cos, sin], rspec, [BF16], tm, "proj_k_ret")
    (v_r,) = _proj(h, w_in, 5, _proj_plain_body, [], [], [BF16], tm, "proj_v_ret")
    (g_r,) = _proj(h, w_in, 6, _proj_plain_body, [], [], [F32], tm, "proj_g_ret")
    return q_da, k_da, k_da_b, v_da, v_da_b, v_da_t, q_r, k_r, v_r, g_r


def _diff_lambda(lq1, lk1, lq2, lk2):
    s1 = jnp.sum(lq1[...] * lk1[...], axis=-1, keepdims=True)
    s2 = jnp.sum(lq2[...] * lk2[...], axis=-1, keepdims=True)
    return jnp.exp(s1) - jnp.exp(s2) + LAM_INIT


def _softmax_step(q, k, v, m_ref, l_ref, acc_ref, idx, mask):
    s = lax.dot_general(q, k, (((1,), (1,)), ((), ())), preferred_element_type=F32)
    if mask is not None:
        s = jnp.where(mask, s, NEG_INF)
    m_prev = m_ref[idx]
    m_new = jnp.maximum(m_prev, jnp.max(s, axis=-1, keepdims=True))
    alpha = jnp.exp(m_prev - m_new)
    p = jnp.exp(s - m_new)
    l_ref[idx] = alpha * l_ref[idx] + jnp.sum(p, axis=-1, keepdims=True)
    acc_ref[idx] = alpha * acc_ref[idx] + jnp.dot(p.astype(BF16), v, preferred_element_type=F32)
    m_ref[idx] = m_new


def _softmax_init(m_ref, l_ref, acc_ref):
    m_ref[...] = jnp.full(m_ref.shape, NEG_INF, F32)
    l_ref[...] = jnp.zeros(l_ref.shape, F32)
    acc_ref[...] = jnp.zeros(acc_ref.shape, F32)


def _diff_combine(lam, gain, l_ref, acc_ref, i0, i1):
    o = acc_ref[i0] / l_ref[i0] - lam * (acc_ref[i1] / l_ref[i1])
    return _rms(o) * gain * (1.0 - LAM_INIT)


def _da_prompt_body(tq, lq1, lk1, lq2, lk2, gain_ref, q_ref, k_ref, vt_ref, o_ref,
                    m_ref, l_ref, acc_ref):
    qi = pl.program_id(1)
    _softmax_init(m_ref, l_ref, acc_ref)

    def block(start, mask):
        k = k_ref[pl.ds(start, tq), :]
        vt = vt_ref[:, pl.ds(start, tq)]
        sts = []
        for mp in range(2):
            sl = slice(mp * DA_DH, (mp + 1) * DA_DH)
            sts.append(lax.dot_general(k[:, sl], q_ref[:, sl], (((1,), (1,)), ((), ())),
                                       preferred_element_type=F32))
        pts, alphas = [], []
        for mp in range(2):
            st = sts[mp]
            if mask is not None:
                st = jnp.where(mask, st, NEG_INF)
            m_prev = m_ref[mp]
            m_new = jnp.maximum(m_prev, jnp.max(st, axis=0, keepdims=True))
            alpha = jnp.exp(m_prev - m_new)
            pt = jnp.exp(st - m_new)
            l_ref[mp] = alpha * l_ref[mp] + jnp.sum(pt, axis=0, keepdims=True)
            m_ref[mp] = m_new
            pts.append(pt.astype(BF16))
            alphas.append(alpha)
        for mp in range(2):
            acc_ref[mp] = alphas[mp] * acc_ref[mp] + jnp.dot(vt, pts[mp],
                                                             preferred_element_type=F32)

    def full_block(kb, carry):
        block(pl.multiple_of(kb * tq, tq), None)
        return carry

    lax.fori_loop(0, qi, full_block, 0)
    kc = lax.shift_right_logical(lax.broadcasted_iota(jnp.int32, (tq, tq), 0), 6)
    qc = lax.shift_right_logical(lax.broadcasted_iota(jnp.int32, (tq, tq), 1), 6)
    block(pl.multiple_of(qi * tq, tq), kc <= qc)

    lam = _diff_lambda(lq1, lk1, lq2, lk2)
    ot = acc_ref[0] / l_ref[0] - lam * (acc_ref[1] / l_ref[1])
    ot = ot * lax.rsqrt(jnp.mean(ot * ot, axis=0, keepdims=True) + EPS)
    ot = ot * gain_ref[...] * (1.0 - LAM_INIT)
    o_ref[...] = ot.T.astype(o_ref.dtype)


def _lambda_args(lq1, lk1, lq2, lk2):
    return [a.reshape(1, DA_DH) for a in (lq1, lk1, lq2, lk2)]


def _da_prompt(q, k, vt, lams, out_gain, tq):
    s = q.shape[0]
    assert CHUNK == 64 and tq % CHUNK == 0 and s % tq == 0
    const = lambda h, i: (0, 0)
    vec = pl.BlockSpec((1, DA_DH), const)
    return pl.pallas_call(
        functools.partial(_da_prompt_body, tq),
        out_shape=jax.ShapeDtypeStruct((s, DA_WIDTH), BF16),
        grid=(DA_HEADS, s // tq),
        in_specs=[vec, vec, vec, vec,
                  pl.BlockSpec((DA_DV, 1), const),
                  pl.BlockSpec((tq, DA_DV), lambda h, i: (i, h)),
                  pl.BlockSpec((s, DA_DV), lambda h, i: (0, h)),
                  pl.BlockSpec((DA_DV, s), lambda h, i: (h, 0))],
        out_specs=pl.BlockSpec((tq, DA_DV), lambda h, i: (i, h)),
        scratch_shapes=[pltpu.VMEM((2, 1, tq), F32), pltpu.VMEM((2, 1, tq), F32),
                        pltpu.VMEM((2, DA_DV, tq), F32)],
        compiler_params=_params("parallel", "parallel"),
        name="diff_attn_prompt",
    )(*lams, out_gain.reshape(DA_DV, 1), q, k, vt)


def _da_sample_body(tk, lq1, lk1, lq2, lk2, gain_ref, q_ref, kn_ref, vn_ref, ck_hbm, cv_hbm, o_ref,
                    kbuf, vbuf, sem, m_ref, l_ref, acc_ref):
    n = pl.program_id(0) * DA_HEADS + pl.program_id(1)
    total = pl.num_programs(0) * DA_HEADS
    slot = lax.rem(n, 2)

    def copies(step, to_slot):
        stream = lax.div(step, DA_HEADS)
        head = lax.rem(step, DA_HEADS)
        return (pltpu.make_async_copy(ck_hbm.at[stream, :, head, :], kbuf.at[to_slot], sem.at[0, to_slot]),
                pltpu.make_async_copy(cv_hbm.at[stream, :, head, :], vbuf.at[to_slot], sem.at[1, to_slot]))

    @pl.when(n == 0)
    def _():
        for c in copies(n, slot):
            c.start()

    @pl.when(n + 1 < total)
    def _():
        for c in copies(n + 1, 1 - slot):
            c.start()

    def attend(k, v):
        for mp in range(2):
            sl = slice(mp * DA_DH, (mp + 1) * DA_DH)
            _softmax_step(q_ref[:, sl], k[:, sl], v, m_ref, l_ref, acc_ref, mp, None)

    _softmax_init(m_ref, l_ref, acc_ref)
    attend(kn_ref[...], vn_ref[...])
    for c in copies(n, slot):
        c.wait()
    for j in range(kbuf.shape[1] // tk):
        attend(kbuf[slot, j * tk:(j + 1) * tk, :].astype(BF16),
               vbuf[slot, j * tk:(j + 1) * tk, :].astype(BF16))
    lam = _diff_lambda(lq1, lk1, lq2, lk2)
    o_ref[...] = _diff_combine(lam, gain_ref[...], l_ref, acc_ref, 0, 1).astype(o_ref.dtype)


def _da_sample(q, k_new, v_new, cache_k, cache_v, lams, out_gain, t, tk):
    b, p = cache_k.shape[:2]
    const = lambda i, h: (0, 0)
    vec = pl.BlockSpec((1, DA_DH), const)
    tok = pl.BlockSpec((t, DA_DV), lambda i, h: (i, h))
    hbm = pl.BlockSpec(memory_space=pl.ANY)
    return pl.pallas_call(
        functools.partial(_da_sample_body, tk),
        out_shape=jax.ShapeDtypeStruct((b * t, DA_WIDTH), BF16),
        grid=(b, DA_HEADS),
        in_specs=[vec, vec, vec, vec, pl.BlockSpec((1, DA_DV), const), tok, tok, tok, hbm, hbm],
        out_specs=tok,
        scratch_shapes=[pltpu.VMEM((2, p, DA_DV), F32), pltpu.VMEM((2, p, DA_DV), F32),
                        pltpu.SemaphoreType.DMA((2, 2)),
                        pltpu.VMEM((2, t, 1), F32), pltpu.VMEM((2, t, 1), F32),
                        pltpu.VMEM((2, t, DA_DV), F32)],
        compiler_params=_params("arbitrary", "arbitrary"),
        name="diff_attn_sample",
    )(*lams, out_gain.reshape(1, DA_DV), q, k_new, v_new, cache_k, cache_v)


def _retention_body(lc, lg_ref, gain_ref, q_ref, k_ref, v_ref, g_ref, s0_ref, o_ref, s_out_ref,
                    state_ref):
    c = pl.program_id(2)

    @pl.when(c == 0)
    def _():
        state_ref[...] = s0_ref[0, 0]

    lg = lg_ref[0][:, :1]
    q = q_ref[...]
    k = k_ref[...]
    v = v_ref[...]
    rel = (lax.broadcasted_iota(jnp.int32, (lc, lc), 0)
           - lax.broadcasted_iota(jnp.int32, (lc, lc), 1)).astype(F32)
    decay = jnp.where(rel >= 0, jnp.exp(lg * jnp.maximum(rel, 0.0)), 0.0)
    scores = lax.dot_general(q, k, (((1,), (1,)), ((), ())), preferred_element_type=F32) * decay
    idx = lax.broadcasted_iota(jnp.int32, (lc, 1), 0).astype(F32)
    state = state_ref[...]
    o = jnp.dot(scores.astype(BF16), v, preferred_element_type=F32)
    o = o + jnp.dot(q, state.astype(BF16), preferred_element_type=F32) * jnp.exp(lg * (idx + 1.0))
    kw = k.astype(F32) * jnp.exp(lg * (lc - 1.0 - idx))
    s_new = state * jnp.exp(lg * lc) + jnp.dot(kw.T.astype(BF16), v, preferred_element_type=F32)
    state_ref[...] = s_new

    g = g_ref[...]
    o_ref[...] = (_rms(o) * gain_ref[...] * (g * _sigmoid(g))).astype(o_ref.dtype)

    @pl.when(c == pl.num_programs(2) - 1)
    def _():
        s_out_ref[0, 0] = s_new


def _retention(q, k, v, g, s0, out_gain, seq, lc):
    b = s0.shape[0]
    nc = seq // lc
    log_gamma = jnp.log(1.0 - 2.0 ** (-5.0 - jnp.arange(RET_HEADS, dtype=F32)))
    lg = jnp.broadcast_to(log_gamma.reshape(RET_HEADS, 1, 1), (RET_HEADS, 1, LANES))
    tok = pl.BlockSpec((lc, RET_DV), lambda bi, h, c: (bi * nc + c, h))
    st = pl.BlockSpec((1, 1, RET_DK, RET_DV), lambda bi, h, c: (bi, h, 0, 0))
    return pl.pallas_call(
        functools.partial(_retention_body, lc),
        out_shape=(jax.ShapeDtypeStruct((b * seq, RET_WIDTH), BF16),
                   jax.ShapeDtypeStruct(s0.shape, F32)),
        grid=(b, RET_HEADS, nc),
        in_specs=[pl.BlockSpec((1, 1, LANES), lambda bi, h, c: (h, 0, 0)),
                  pl.BlockSpec((1, RET_DV), lambda bi, h, c: (0, 0)),
                  tok, tok, tok, tok, st],
        out_specs=(tok, st),
        scratch_shapes=[pltpu.VMEM((RET_DK, RET_DV), F32)],
        compiler_params=_params("parallel", "parallel", "arbitrary"),
        name="retention",
    )(lg, out_gain.reshape(1, RET_DV), q, k, v, g, s0)


def _route(logits):
    lane = lax.broadcasted_iota(jnp.int32, logits.shape, 1)
    big = jnp.int32(LANES)
    neg = -jnp.inf
    gl = jnp.where((lane >= N_EXPERTS) & (lane < N_EXPERTS + N_GROUPS), logits, neg)
    g_max = jnp.max(gl, axis=-1, keepdims=True)
    g_idx = jnp.min(jnp.where(gl == g_max, lane - N_EXPERTS, big), axis=-1, keepdims=True)
    g_w = 1.0 / jnp.sum(jnp.exp(gl - g_max), axis=-1, keepdims=True)
    in_group = (lane < N_EXPERTS) & (lax.shift_right_logical(lane, 3) == g_idx)
    el = jnp.where(in_group, logits, neg)
    v1 = jnp.max(el, axis=-1, keepdims=True)
    i1 = jnp.min(jnp.where(el == v1, lane, big), axis=-1, keepdims=True)
    el2 = jnp.where(lane == i1, neg, el)
    v2 = jnp.max(el2, axis=-1, keepdims=True)
    i2 = jnp.min(jnp.where(el2 == v2, lane, big), axis=-1, keepdims=True)
    e2 = jnp.exp(v2 - v1)
    w1 = g_w / (1.0 + e2)
    w2 = g_w * e2 / (1.0 + e2)
    return jnp.where(lane == i1, w1, 0.0) + jnp.where(lane == i2, w2, 0.0)


def _outproj_body(x_ref, oda_ref, ort_ref, wo_ref, gf_ref, wr_hi_ref, wr_lo_ref, br_ref,
                  x1_ref, hf_ref, gate_ref):
    x1 = (x_ref[...]
          + jnp.dot(oda_ref[...], wo_ref[:DA_WIDTH, :], preferred_element_type=F32)
          + jnp.dot(ort_ref[...], wo_ref[DA_WIDTH:, :], preferred_element_type=F32))
    x1_ref[...] = x1
    hf = _rms(x1) * gf_ref[...]
    hf_hi = hf.astype(BF16)
    hf_ref[...] = hf
    hf_lo = (hf - hf_hi.astype(F32)).astype(BF16)
    wr_hi = wr_hi_ref[...]
    logits = (jnp.dot(hf_hi, wr_hi, preferred_element_type=F32)
              + jnp.dot(hf_lo, wr_hi, preferred_element_type=F32)
              + jnp.dot(hf_hi, wr_lo_ref[...], preferred_element_type=F32)
              + br_ref[...])
    gate_ref[...] = _route(logits)


def _outproj(x, o_da, o_ret, w_out, ffn_g, wr_hi, wr_lo, br, tm):
    t, d = x.shape
    row = lambda i: (i, 0)
    const = lambda i: (0, 0)
    return pl.pallas_call(
        _outproj_body,
        out_shape=(jax.ShapeDtypeStruct((t, d), F32), jax.ShapeDtypeStruct((t, d), F32),
                   jax.ShapeDtypeStruct((t, LANES), F32)),
        grid=(t // tm,),
        in_specs=[pl.BlockSpec((tm, d), row),
                  pl.BlockSpec((tm, DA_WIDTH), row),
                  pl.BlockSpec((tm, RET_WIDTH), row),
                  pl.BlockSpec((d, d), const),
                  pl.BlockSpec((1, d), const),
                  pl.BlockSpec((d, LANES), const),
                  pl.BlockSpec((d, LANES), const),
                  pl.BlockSpec((1, LANES), const)],
        out_specs=(pl.BlockSpec((tm, d), row), pl.BlockSpec((tm, d), row),
                   pl.BlockSpec((tm, LANES), row)),
        compiler_params=_params("parallel"),
        name="out_proj_router",
    )(x, o_da, o_ret, w_out, ffn_g.reshape(1, d), wr_hi, wr_lo, br)


MOE_TILE = 256


def _row_copy(src_hbm, row, dst, dst_row, sem):
    return pltpu.make_async_copy(src_hbm.at[pl.ds(row, 1), :], dst.at[pl.ds(dst_row, 1), :], sem)


def _moe_expert_body(src_ref, texp_ref, nused_ref, hf_hbm, wg_ref, wu_ref, wd_ref, ys_ref,
                     xbuf, sem):
    j = pl.program_id(0)
    slot = lax.rem(j, 2)
    nused = nused_ref[0]

    def gather(tile, to_slot, start):
        def two(r2, carry):
            for k in range(2):
                r = 2 * r2 + k
                c = _row_copy(hf_hbm, src_ref[tile * MOE_TILE + r] if start else 0,
                              xbuf.at[to_slot], r, sem.at[to_slot])
                c.start(priority=k) if start else c.wait()
            return carry
        lax.fori_loop(0, MOE_TILE // 2, two, 0, unroll=4)

    @pl.when(j == 0)
    def _():
        gather(j, slot, True)

    @pl.when(j + 1 < nused)
    def _():
        gather(j + 1, 1 - slot, True)

    @pl.when(j < nused)
    def _():
        gather(j, slot, False)
        x = xbuf[slot].astype(BF16)
        a = jnp.dot(x, wg_ref[0], preferred_element_type=F32)
        u = jnp.dot(x, wu_ref[0], preferred_element_type=F32)
        act = (a * _sigmoid(a)) * u
        ys_ref[...] = jnp.dot(act.astype(BF16), wd_ref[0], preferred_element_type=F32)

    @pl.when(j >= nused)
    def _():
        ys_ref[...] = jnp.zeros(ys_ref.shape, F32)


def _moe_combine_body(pos_ref, x1_ref, w_ref, ys_hbm, y_ref, gbuf, sem):
    i = pl.program_id(0)
    slot = lax.rem(i, 2)
    tc = x1_ref.shape[0]

    def gather(tile, to_slot, start):
        def one(r, carry):
            for k in range(2):
                c = _row_copy(ys_hbm, pos_ref[2 * (tile * tc + r) + k] if start else 0,
                              gbuf.at[to_slot, k], r, sem.at[to_slot])
                c.start(priority=k) if start else c.wait()
            return carry
        lax.fori_loop(0, tc, one, 0, unroll=4)

    @pl.when(i == 0)
    def _():
        gather(i, slot, True)

    @pl.when(i + 1 < pl.num_programs(0))
    def _():
        gather(i + 1, 1 - slot, True)

    gather(i, slot, False)
    w = w_ref[...]
    y_ref[...] = x1_ref[...] + w[:, 0:1] * gbuf[slot, 0] + w[:, 1:2] * gbuf[slot, 1]


def _moe_dense_body(hf_ref, gate_ref, x1_ref, wg_ref, wu_ref, wd_ref, y_ref):
    e = pl.program_id(1)

    @pl.when(e == 0)
    def _():
        y_ref[...] = x1_ref[...]

    h = hf_ref[...].astype(BF16)
    a = jnp.dot(h, wg_ref[0], preferred_element_type=F32)
    u = jnp.dot(h, wu_ref[0], preferred_element_type=F32)
    gate = gate_ref[...]
    lane = lax.broadcasted_iota(jnp.int32, gate.shape, 1)
    ge = jnp.sum(jnp.where(lane == e, gate, 0.0), axis=-1, keepdims=True)
    act = (a * _sigmoid(a)) * u * ge
    y_ref[...] += jnp.dot(act.astype(BF16), wd_ref[0], preferred_element_type=F32)


def _moe_dense(hf, gate, x1, w_gate, w_up, w_down, tm):
    t, d = hf.shape
    row = lambda i, e: (i, 0)
    return pl.pallas_call(
        _moe_dense_body,
        out_shape=jax.ShapeDtypeStruct((t, d), F32),
        grid=(t // tm, N_EXPERTS),
        in_specs=[pl.BlockSpec((tm, d), row),
                  pl.BlockSpec((tm, LANES), row),
                  pl.BlockSpec((tm, d), row),
                  pl.BlockSpec((1, d, D_FF), lambda i, e: (e, 0, 0)),
                  pl.BlockSpec((1, d, D_FF), lambda i, e: (e, 0, 0)),
                  pl.BlockSpec((1, D_FF, d), lambda i, e: (e, 0, 0))],
        out_specs=pl.BlockSpec((tm, d), row),
        compiler_params=_params("parallel", "arbitrary"),
        name="moe_dense",
    )(hf, gate, x1, w_gate, w_up, w_down)


def _moe(hf, gate, x1, w_gate, w_up, w_down, tm):
    t, d = hf.shape
    n_pair = 2 * t
    if n_pair < N_EXPERTS * MOE_TILE:
        return _moe_dense(hf, gate, x1, w_gate, w_up, w_down, tm)
    w_pair, e_pair = lax.top_k(gate[:, :N_EXPERTS], 2)
    e_flat = e_pair.reshape(-1).astype(jnp.int32)
    n_pad = n_pair + N_EXPERTS * MOE_TILE
    n_tiles = n_pad // MOE_TILE
    onehot = (e_flat[:, None] == jnp.arange(N_EXPERTS, dtype=jnp.int32)[None, :]).astype(jnp.int32)
    csum = jnp.cumsum(onehot, axis=0)
    counts = csum[-1]
    rank = jnp.take_along_axis(csum, e_flat[:, None], axis=1)[:, 0] - 1
    padded = ((counts + MOE_TILE - 1) // MOE_TILE) * MOE_TILE
    ends = jnp.cumsum(padded)
    starts = ends - padded
    pos = (starts[e_flat] + rank).astype(jnp.int32)
    tile_start = jnp.arange(n_tiles, dtype=jnp.int32) * MOE_TILE
    tile_expert = jnp.minimum(jnp.searchsorted(ends, tile_start, side="right"),
                              N_EXPERTS - 1).astype(jnp.int32)
    n_used = (ends[-1] // MOE_TILE).astype(jnp.int32).reshape(1)
    order = jnp.argsort(e_flat, stable=True).astype(jnp.int32)
    slot_e = jnp.repeat(tile_expert, MOE_TILE)
    slot_r = jnp.arange(n_pad, dtype=jnp.int32) - starts[slot_e]
    sorted_idx = jnp.clip((jnp.cumsum(counts) - counts)[slot_e] + slot_r, 0, n_pair - 1)
    src_tok = jnp.where(slot_r < counts[slot_e], order[sorted_idx] // 2, 0).astype(jnp.int32)

    wspec = lambda shape: pl.BlockSpec(shape, lambda j, src, te, nu: (te[j], 0, 0))
    ys = pl.pallas_call(
        _moe_expert_body,
        out_shape=jax.ShapeDtypeStruct((n_pad, d), F32),
        grid_spec=pltpu.PrefetchScalarGridSpec(
            num_scalar_prefetch=3,
            grid=(n_tiles,),
            in_specs=[pl.BlockSpec(memory_space=pl.ANY),
                      wspec((1, d, D_FF)), wspec((1, d, D_FF)), wspec((1, D_FF, d))],
            out_specs=pl.BlockSpec((MOE_TILE, d), lambda j, src, te, nu: (j, 0)),
            scratch_shapes=[pltpu.VMEM((2, MOE_TILE, d), F32), pltpu.SemaphoreType.DMA((2,))]),
        compiler_params=_params("arbitrary"),
        name="moe_experts",
    )(src_tok, tile_expert, n_used, hf, w_gate, w_up, w_down)

    tc = min(tm, MOE_TILE)
    return pl.pallas_call(
        _moe_combine_body,
        out_shape=jax.ShapeDtypeStruct((t, d), F32),
        grid_spec=pltpu.PrefetchScalarGridSpec(
            num_scalar_prefetch=1,
            grid=(t // tc,),
            in_specs=[pl.BlockSpec((tc, d), lambda i, p: (i, 0)),
                      pl.BlockSpec((tc, 2), lambda i, p: (i, 0)),
                      pl.BlockSpec(memory_space=pl.ANY)],
            out_specs=pl.BlockSpec((tc, d), lambda i, p: (i, 0)),
            scratch_shapes=[pltpu.VMEM((2, 2, tc, d), F32), pltpu.SemaphoreType.DMA((2,))]),
        compiler_params=_params("arbitrary"),
        name="moe_combine",
    )(pos, x1, w_pair, ys)


def _split_hi_lo(w):
    hi = w.astype(BF16)
    return hi, (w - hi.astype(F32)).astype(BF16)


def kernel(x_prompt, x_sample, cache_k_diff, cache_v_diff, state_retention, attn_norm_g, w_in, da_q_norm_g, da_k_norm_g, da_lambda_q1, da_lambda_k1, da_lambda_q2, da_lambda_k2, da_out_norm_g, ret_out_norm_g, w_out, ffn_norm_g, w_group, b_group, w_expert, b_expert, w_gate, w_up, w_down):
    assert w_in.shape[0] == 1, "single-layer model"
    bp, seq, d = x_prompt.shape
    bd, t_dec, _ = x_sample.shape
    past = cache_k_diff.shape[2]
    assert bp == 1

    w_in_b = w_in[0].astype(BF16)
    w_out_b = w_out[0].astype(BF16)
    w_gate_b = w_gate[0].astype(BF16)
    w_up_b = w_up[0].astype(BF16)
    w_down_b = w_down[0].astype(BF16)
    w_router = jnp.concatenate([w_expert[0].reshape(d, N_EXPERTS), w_group[0]], axis=1)
    w_router = jnp.pad(w_router, ((0, 0), (0, LANES - w_router.shape[1])))
    wr_hi, wr_lo = _split_hi_lo(w_router)
    b_router = jnp.concatenate([b_expert[0].reshape(N_EXPERTS), b_group[0]])
    b_router = jnp.pad(b_router, (0, LANES - b_router.shape[0])).reshape(1, LANES)

    lams = _lambda_args(da_lambda_q1[0], da_lambda_k1[0], da_lambda_q2[0], da_lambda_k2[0])

    def layer(x, cos, sin, attend, want_vt, ret_seq, ret_chunk, s0, tm):
        h = _rmsnorm(x, attn_norm_g[0], tm)
        q_da, k_da, k_da_b, v_da, v_da_b, v_da_t, q_r, k_r, v_r, g_r = _project(
            h, w_in_b, da_q_norm_g[0], da_k_norm_g[0], cos, sin, tm, want_vt)
        o_da = attend(q_da, k_da_b, v_da_b, v_da_t)
        o_ret, s_new = _retention(q_r, k_r, v_r, g_r, s0, ret_out_norm_g[0], ret_seq, ret_chunk)
        x1, hf, gate = _outproj(x, o_da, o_ret, w_out_b, ffn_norm_g[0], wr_hi, wr_lo, b_router,
                                min(tm, 256))
        y = _moe(hf, gate, x1, w_gate_b, w_up_b, w_down_b, tm)
        return y, k_da, v_da, s_new

    cos_p, sin_p = _rope_tables(seq, 0)
    y_p, k_p, v_p, s_p = layer(
        x_prompt.reshape(seq, d), cos_p, sin_p,
        lambda q, k, v, vt: _da_prompt(q, k, vt[0], lams, da_out_norm_g[0], 512),
        True, seq, 256, jnp.zeros((bp, RET_HEADS, RET_DK, RET_DV), F32), 512)

    cos_s, sin_s = _rope_tables(t_dec, past)
    cos_s = jnp.tile(cos_s, (bd, 1))
    sin_s = jnp.tile(sin_s, (bd, 1))
    y_s, k_s, v_s, s_s = layer(
        x_sample.reshape(bd * t_dec, d), cos_s, sin_s,
        lambda q, k, v, vt: _da_sample(q, k, v, cache_k_diff[0], cache_v_diff[0], lams,
                                       da_out_norm_g[0], t_dec, 1024),
        False, t_dec, t_dec, state_retention[0], bd * t_dec)

    return (y_p.reshape(bp, seq, d),
            y_s.reshape(bd, t_dec, d),
            k_p.reshape(1, bp, seq, DA_HEADS, 2 * DA_DH),
            v_p.reshape(1, bp, seq, DA_HEADS, DA_DV),
            s_p.reshape(1, bp, RET_HEADS, RET_DK, RET_DV),
            k_s.reshape(1, bd, t_dec, DA_HEADS, 2 * DA_DH),
            v_s.reshape(1, bd, t_dec, DA_HEADS, DA_DV),
            s_s.reshape(1, bd, RET_HEADS, RET_DK, RET_DV))
```

```python
import functools
import math

import jax
import jax.numpy as jnp
from jax import lax
from jax.experimental import pallas as pl
from jax.experimental.pallas import tpu as pltpu

D_MODEL = 2048
CHUNK = 64
DA_HEADS = 4
DA_DH = 128
DA_DV = 2 * DA_DH
DA_WIDTH = DA_HEADS * DA_DV
RET_HEADS = 4
RET_DK = 256
RET_DV = 256
RET_WIDTH = RET_HEADS * RET_DV
IN_GROUP = 1024
N_GROUPS = 4
EXP_PER_GROUP = 8
N_EXPERTS = N_GROUPS * EXP_PER_GROUP
D_FF = D_MODEL // 8
EPS = 1e-6
NEG_INF = -1e30
ROPE_BASE = 10000.0
LAM_INIT = 0.8 - 0.6 * math.exp(-0.3 * 0)

LANES = 128
VMEM_LIMIT = 48 * 1024 * 1024

F32 = jnp.float32
BF16 = jnp.bfloat16


def _params(*sem):
    return pltpu.CompilerParams(dimension_semantics=sem, vmem_limit_bytes=VMEM_LIMIT)


def _sigmoid(x):
    return 1.0 / (1.0 + jnp.exp(-x))


def _rms(x):
    return x * lax.rsqrt(jnp.mean(x * x, axis=-1, keepdims=True) + EPS)


def _rmsnorm_body(x_ref, g_ref, o_ref):
    o_ref[...] = (_rms(x_ref[...]) * g_ref[...]).astype(o_ref.dtype)


def _rmsnorm(x, g, tm):
    t, d = x.shape
    return pl.pallas_call(
        _rmsnorm_body,
        out_shape=jax.ShapeDtypeStruct((t, d), BF16),
        grid=(t // tm,),
        in_specs=[pl.BlockSpec((tm, d), lambda i: (i, 0)),
                  pl.BlockSpec((1, d), lambda i: (0, 0))],
        out_specs=pl.BlockSpec((tm, d), lambda i: (i, 0)),
        compiler_params=_params("parallel"),
        name="attn_norm",
    )(x, g.reshape(1, d))


def _rope_table_body(pos0, tr, invf_ref, cos_ref, sin_ref):
    row = lax.broadcasted_iota(jnp.int32, (tr, LANES), 0) + (pl.program_id(0) * tr + pos0)
    ang = row.astype(F32) * invf_ref[...]
    cos_ref[...] = jnp.cos(ang)
    sin_ref[...] = jnp.sin(ang)


def _rope_tables(n_pos, pos0):
    half = RET_DK // 2
    inv_freq = (ROPE_BASE ** (-jnp.arange(half, dtype=F32) / half)).reshape(1, half)
    tr = min(n_pos, 512)
    spec = pl.BlockSpec((tr, half), lambda i: (i, 0))
    return pl.pallas_call(
        functools.partial(_rope_table_body, pos0, tr),
        out_shape=(jax.ShapeDtypeStruct((n_pos, half), F32),) * 2,
        grid=(n_pos // tr,),
        in_specs=[pl.BlockSpec((1, half), lambda i: (0, 0))],
        out_specs=(spec, spec),
        compiler_params=_params("parallel"),
        name="rope_tables",
    )(inv_freq)


def _store_cols(o, sl, val):
    if len(o.shape) == 2:
        o[:, sl] = val.astype(o.dtype)
    else:
        hd, off = divmod(sl.start, DA_DV)
        o[:, hd, off:off + (sl.stop - sl.start)] = val.astype(o.dtype)


def _store_all(z, outs):
    for o in outs:
        for hd in range(DA_HEADS):
            sl = slice(hd * DA_DV, (hd + 1) * DA_DV)
            _store_cols(o, sl, z[:, sl])


def _proj_plain_body(h_ref, w_ref, *outs):
    z = jnp.dot(h_ref[...], w_ref[...], preferred_element_type=F32)
    _store_all(z, outs)


def _proj_plain_t_body(h_ref, w_ref, *outs):
    z = jnp.dot(h_ref[...], w_ref[...], preferred_element_type=F32)
    _store_all(z, outs[:-1])
    outs[-1][...] = z.T.astype(outs[-1].dtype)


def _proj_qknorm_body(scale, h_ref, w_ref, g_ref, *outs):
    z = jnp.dot(h_ref[...], w_ref[...], preferred_element_type=F32)
    g = g_ref[...]
    for c in range(IN_GROUP // DA_DH):
        sl = slice(c * DA_DH, (c + 1) * DA_DH)
        zc = _rms(z[:, sl]) * g
        for o in outs:
            _store_cols(o, sl, zc * scale if o.dtype == BF16 else zc)


def _proj_rotary_body(scale, h_ref, w_ref, cos_ref, sin_ref, o_ref):
    z = jnp.dot(h_ref[...], w_ref[...], preferred_element_type=F32) * scale
    cos = cos_ref[...]
    sin = sin_ref[...]
    half = RET_DK // 2
    for hd in range(RET_HEADS):
        x1 = z[:, hd * RET_DK: hd * RET_DK + half]
        x2 = z[:, hd * RET_DK + half: (hd + 1) * RET_DK]
        o_ref[:, hd * RET_DK: hd * RET_DK + half] = (x1 * cos - x2 * sin).astype(o_ref.dtype)
        o_ref[:, hd * RET_DK + half: (hd + 1) * RET_DK] = (x1 * sin + x2 * cos).astype(o_ref.dtype)


def _proj(h, w_in, group, body, extra, extra_specs, out_dtypes, tm, name, per_head=False,
          transposed_out=False):
    t, d = h.shape
    row = lambda i: (i, 0)
    split = lambda dt: per_head and dt == F32
    shape = lambda dt: (t, DA_HEADS, DA_DV) if split(dt) else (t, IN_GROUP)
    block = lambda dt: (pl.BlockSpec((tm, DA_HEADS, DA_DV), lambda i: (i, 0, 0)) if split(dt)
                        else pl.BlockSpec((tm, IN_GROUP), row))
    outs = tuple(jax.ShapeDtypeStruct(shape(dt), dt) for dt in out_dtypes)
    out_specs = tuple(block(dt) for dt in out_dtypes)
    if transposed_out:
        outs += (jax.ShapeDtypeStruct((IN_GROUP, t), BF16),)
        out_specs += (pl.BlockSpec((IN_GROUP, tm), lambda i: (0, i)),)
    res = pl.pallas_call(
        body,
        out_shape=outs,
        grid=(t // tm,),
        in_specs=[pl.BlockSpec((tm, d), row),
                  pl.BlockSpec((d, IN_GROUP), lambda i: (0, group))] + extra_specs,
        out_specs=out_specs,
        compiler_params=_params("parallel"),
        name=name,
    )(h, w_in, *extra)
    return res


def _project(h, w_in, q_norm_g, k_norm_g, cos, sin, tm, want_vt):
    half = RET_DK // 2
    row = lambda i: (i, 0)
    gspec = [pl.BlockSpec((1, DA_DH), lambda i: (0, 0))]
    rspec = [pl.BlockSpec((tm, half), row), pl.BlockSpec((tm, half), row)]
    (q_da,) = _proj(h, w_in, 0, functools.partial(_proj_qknorm_body, DA_DH ** -0.5),
                    [q_norm_g.reshape(1, DA_DH)], gspec, [BF16], tm, "proj_q_da")
    k_da, k_da_b = _proj(h, w_in, 1, functools.partial(_proj_qknorm_body, 1.0),
                         [k_norm_g.reshape(1, DA_DH)], gspec, [F32, BF16], tm, "proj_k_da",
                         per_head=True)
    v_da, v_da_b, *v_da_t = _proj(h, w_in, 2, _proj_plain_t_body if want_vt else _proj_plain_body,
                                  [], [], [F32, BF16], tm, "proj_v_da", per_head=True,
                                  transposed_out=want_vt)
    (q_r,) = _proj(h, w_in, 3, functools.partial(_proj_rotary_body, 1.0),
                   [cos, sin], rspec, [BF16], tm, "proj_q_ret")
    (k_r,) = _proj(h, w_in, 4, functools.partial(_proj_rotary_body, RET_DK ** -0.5),
                   [cos, sin], rspec, [BF16], tm, "proj_k_ret")
    (v_r,) = _proj(h, w_in, 5, _proj_plain_body, [], [], [BF16], tm, "proj_v_ret")
    (g_r,) = _proj(h, w_in, 6, _proj_plain_body, [], [], [F32], tm, "proj_g_ret")
    return q_da, k_da, k_da_b, v_da, v_da_b, v_da_t, q_r, k_r, v_r, g_r


def _diff_lambda(lq1, lk1, lq2, lk2):
    s1 = jnp.sum(lq1[...] * lk1[...], axis=-1, keepdims=True)
    s2 = jnp.sum(lq2[...] * lk2[...], axis=-1, keepdims=True)
    return jnp.exp(s1) - jnp.exp(s2) + LAM_INIT


def _softmax_step(q, k, v, m_ref, l_ref, acc_ref, idx, mask):
    s = lax.dot_general(q, k, (((1,), (1,)), ((), ())), preferred_element_type=F32)
    if mask is not None:
        s = jnp.where(mask, s, NEG_INF)
    m_prev = m_ref[idx]
    m_new = jnp.maximum(m_prev, jnp.max(s, axis=-1, keepdims=True))
    alpha = jnp.exp(m_prev - m_new)
    p = jnp.exp(s - m_new)
    l_ref[idx] = alpha * l_ref[idx] + jnp.sum(p, axis=-1, keepdims=True)
    acc_ref[idx] = alpha * acc_ref[idx] + jnp.dot(p.astype(BF16), v, preferred_element_type=F32)
    m_ref[idx] = m_new


def _softmax_init(m_ref, l_ref, acc_ref):
    m_ref[...] = jnp.full(m_ref.shape, NEG_INF, F32)
    l_ref[...] = jnp.zeros(l_ref.shape, F32)
    acc_ref[...] = jnp.zeros(acc_ref.shape, F32)


def _diff_combine(lam, gain, l_ref, acc_ref, i0, i1):
    o = acc_ref[i0] / l_ref[i0] - lam * (acc_ref[i1] / l_ref[i1])
    return _rms(o) * gain * (1.0 - LAM_INIT)


def _da_prompt_body(tq, lq1, lk1, lq2, lk2, gain_ref, q_ref, k_ref, vt_ref, o_ref,
                    m_ref, l_ref, acc_ref):
    qi = pl.program_id(1)
    _softmax_init(m_ref, l_ref, acc_ref)

    def block(start, mask):
        k = k_ref[pl.ds(start, tq), :]
        vt = vt_ref[:, pl.ds(start, tq)]
        sts = []
        for mp in range(2):
            sl = slice(mp * DA_DH, (mp + 1) * DA_DH)
            sts.append(lax.dot_general(k[:, sl], q_ref[:, sl], (((1,), (1,)), ((), ())),
                                       preferred_element_type=F32))
        pts, alphas = [], []
        for mp in range(2):
            st = sts[mp]
            if mask is not None:
                st = jnp.where(mask, st, NEG_INF)
            m_prev = m_ref[mp]
            m_new = jnp.maximum(m_prev, jnp.max(st, axis=0, keepdims=True))
            alpha = jnp.exp(m_prev - m_new)
            pt = jnp.exp(st - m_new)
            l_ref[mp] = alpha * l_ref[mp] + jnp.sum(pt, axis=0, keepdims=True)
            m_ref[mp] = m_new
            pts.append(pt.astype(BF16))
            alphas.append(alpha)
        for mp in range(2):
            acc_ref[mp] = alphas[mp] * acc_ref[mp] + jnp.dot(vt, pts[mp],
                                                             preferred_element_type=F32)

    def full_block(kb, carry):
        block(pl.multiple_of(kb * tq, tq), None)
        return carry

    lax.fori_loop(0, qi, full_block, 0)
    kc = lax.shift_right_logical(lax.broadcasted_iota(jnp.int32, (tq, tq), 0), 6)
    qc = lax.shift_right_logical(lax.broadcasted_iota(jnp.int32, (tq, tq), 1), 6)
    block(pl.multiple_of(qi * tq, tq), kc <= qc)

    lam = _diff_lambda(lq1, lk1, lq2, lk2)
    ot = acc_ref[0] / l_ref[0] - lam * (acc_ref[1] / l_ref[1])
    ot = ot * lax.rsqrt(jnp.mean(ot * ot, axis=0, keepdims=True) + EPS)
    ot = ot * gain_ref[...] * (1.0 - LAM_INIT)
    o_ref[...] = ot.T.astype(o_ref.dtype)


def _lambda_args(lq1, lk1, lq2, lk2):
    return [a.reshape(1, DA_DH) for a in (lq1, lk1, lq2, lk2)]


def _da_prompt(q, k, vt, lams, out_gain, tq):
    s = q.shape[0]
    assert CHUNK == 64 and tq % CHUNK == 0 and s % tq == 0
    const = lambda h, i: (0, 0)
    vec = pl.BlockSpec((1, DA_DH), const)
    return pl.pallas_call(
        functools.partial(_da_prompt_body, tq),
        out_shape=jax.ShapeDtypeStruct((s, DA_WIDTH), BF16),
        grid=(DA_HEADS, s // tq),
        in_specs=[vec, vec, vec, vec,
                  pl.BlockSpec((DA_DV, 1), const),
                  pl.BlockSpec((tq, DA_DV), lambda h, i: (i, h)),
                  pl.BlockSpec((s, DA_DV), lambda h, i: (0, h)),
                  pl.BlockSpec((DA_DV, s), lambda h, i: (h, 0))],
        out_specs=pl.BlockSpec((tq, DA_DV), lambda h, i: (i, h)),
        scratch_shapes=[pltpu.VMEM((2, 1, tq), F32), pltpu.VMEM((2, 1, tq), F32),
                        pltpu.VMEM((2, DA_DV, tq), F32)],
        compiler_params=_params("parallel", "parallel"),
        name="diff_attn_prompt",
    )(*lams, out_gain.reshape(DA_DV, 1), q, k, vt)


def _da_sample_body(tk, lq1, lk1, lq2, lk2, gain_ref, q_ref, kn_ref, vn_ref, ck_hbm, cv_hbm, o_ref,
                    kbuf, vbuf, sem, m_ref, l_ref, acc_ref):
    n = pl.program_id(0) * DA_HEADS + pl.program_id(1)
    total = pl.num_programs(0) * DA_HEADS
    slot = lax.rem(n, 2)

    def copies(step, to_slot):
        stream = lax.div(step, DA_HEADS)
        head = lax.rem(step, DA_HEADS)
        return (pltpu.make_async_copy(ck_hbm.at[stream, :, head, :], kbuf.at[to_slot], sem.at[0, to_slot]),
                pltpu.make_async_copy(cv_hbm.at[stream, :, head, :], vbuf.at[to_slot], sem.at[1, to_slot]))

    @pl.when(n == 0)
    def _():
        for c in copies(n, slot):
            c.start()

    @pl.when(n + 1 < total)
    def _():
        for c in copies(n + 1, 1 - slot):
            c.start()

    def attend(k, v):
        for mp in range(2):
            sl = slice(mp * DA_DH, (mp + 1) * DA_DH)
            _softmax_step(q_ref[:, sl], k[:, sl], v, m_ref, l_ref, acc_ref, mp, None)

    _softmax_init(m_ref, l_ref, acc_ref)
    attend(kn_ref[...], vn_ref[...])
    for c in copies(n, slot):
        c.wait()
    for j in range(kbuf.shape[1] // tk):
        attend(kbuf[slot, j * tk:(j + 1) * tk, :].astype(BF16),
               vbuf[slot, j * tk:(j + 1) * tk, :].astype(BF16))
    lam = _diff_lambda(lq1, lk1, lq2, lk2)
    o_ref[...] = _diff_combine(lam, gain_ref[...], l_ref, acc_ref, 0, 1).astype(o_ref.dtype)


def _da_sample(q, k_new, v_new, cache_k, cache_v, lams, out_gain, t, tk):
    b, p = cache_k.shape[:2]
    const = lambda i, h: (0, 0)
    vec = pl.BlockSpec((1, DA_DH), const)
    tok = pl.BlockSpec((t, DA_DV), lambda i, h: (i, h))
    hbm = pl.BlockSpec(memory_space=pl.ANY)
    return pl.pallas_call(
        functools.partial(_da_sample_body, tk),
        out_shape=jax.ShapeDtypeStruct((b * t, DA_WIDTH), BF16),
        grid=(b, DA_HEADS),
        in_specs=[vec, vec, vec, vec, pl.BlockSpec((1, DA_DV), const), tok, tok, tok, hbm, hbm],
        out_specs=tok,
        scratch_shapes=[pltpu.VMEM((2, p, DA_DV), F32), pltpu.VMEM((2, p, DA_DV), F32),
                        pltpu.SemaphoreType.DMA((2, 2)),
                        pltpu.VMEM((2, t, 1), F32), pltpu.VMEM((2, t, 1), F32),
                        pltpu.VMEM((2, t, DA_DV), F32)],
        compiler_params=_params("arbitrary", "arbitrary"),
        name="diff_attn_sample",
    )(*lams, out_gain.reshape(1, DA_DV), q, k_new, v_new, cache_k, cache_v)


def _retention_body(lc, lg_ref, gain_ref, q_ref, k_ref, v_ref, g_ref, s0_ref, o_ref, s_out_ref,
                    state_ref):
    c = pl.program_id(2)

    @pl.when(c == 0)
    def _():
        state_ref[...] = s0_ref[0, 0]

    lg = lg_ref[0][:, :1]
    q = q_ref[...]
    k = k_ref[...]
    v = v_ref[...]
    rel = (lax.broadcasted_iota(jnp.int32, (lc, lc), 0)
           - lax.broadcasted_iota(jnp.int32, (lc, lc), 1)).astype(F32)
    decay = jnp.where(rel >= 0, jnp.exp(lg * jnp.maximum(rel, 0.0)), 0.0)
    scores = lax.dot_general(q, k, (((1,), (1,)), ((), ())), preferred_element_type=F32) * decay
    idx = lax.broadcasted_iota(jnp.int32, (lc, 1), 0).astype(F32)
    state = state_ref[...]
    o = jnp.dot(scores.astype(BF16), v, preferred_element_type=F32)
    o = o + jnp.dot(q, state.astype(BF16), preferred_element_type=F32) * jnp.exp(lg * (idx + 1.0))
    kw = k.astype(F32) * jnp.exp(lg * (lc - 1.0 - idx))
    s_new = state * jnp.exp(lg * lc) + jnp.dot(kw.T.astype(BF16), v, preferred_element_type=F32)
    state_ref[...] = s_new

    g = g_ref[...]
    o_ref[...] = (_rms(o) * gain_ref[...] * (g * _sigmoid(g))).astype(o_ref.dtype)

    @pl.when(c == pl.num_programs(2) - 1)
    def _():
        s_out_ref[0, 0] = s_new


def _retention(q, k, v, g, s0, out_gain, seq, lc):
    b = s0.shape[0]
    nc = seq // lc
    log_gamma = jnp.log(1.0 - 2.0 ** (-5.0 - jnp.arange(RET_HEADS, dtype=F32)))
    lg = jnp.broadcast_to(log_gamma.reshape(RET_HEADS, 1, 1), (RET_HEADS, 1, LANES))
    tok = pl.BlockSpec((lc, RET_DV), lambda bi, h, c: (bi * nc + c, h))
    st = pl.BlockSpec((1, 1, RET_DK, RET_DV), lambda bi, h, c: (bi, h, 0, 0))
    return pl.pallas_call(
        functools.partial(_retention_body, lc),
        out_shape=(jax.ShapeDtypeStruct((b * seq, RET_WIDTH), BF16),
                   jax.ShapeDtypeStruct(s0.shape, F32)),
        grid=(b, RET_HEADS, nc),
        in_specs=[pl.BlockSpec((1, 1, LANES), lambda bi, h, c: (h, 0, 0)),
                  pl.BlockSpec((1, RET_DV), lambda bi, h, c: (0, 0)),
                  tok, tok, tok, tok, st],
        out_specs=(tok, st),
        scratch_shapes=[pltpu.VMEM((RET_DK, RET_DV), F32)],
        compiler_params=_params("parallel", "parallel", "arbitrary"),
        name="retention",
    )(lg, out_gain.reshape(1, RET_DV), q, k, v, g, s0)


def _route(logits):
    lane = lax.broadcasted_iota(jnp.int32, logits.shape, 1)
    big = jnp.int32(LANES)
    neg = -jnp.inf
    gl = jnp.where((lane >= N_EXPERTS) & (lane < N_EXPERTS + N_GROUPS), logits, neg)
    g_max = jnp.max(gl, axis=-1, keepdims=True)
    g_idx = jnp.min(jnp.where(gl == g_max, lane - N_EXPERTS, big), axis=-1, keepdims=True)
    g_w = 1.0 / jnp.sum(jnp.exp(gl - g_max), axis=-1, keepdims=True)
    in_group = (lane < N_EXPERTS) & (lax.shift_right_logical(lane, 3) == g_idx)
    el = jnp.where(in_group, logits, neg)
    v1 = jnp.max(el, axis=-1, keepdims=True)
    i1 = jnp.min(jnp.where(el == v1, lane, big), axis=-1, keepdims=True)
    el2 = jnp.where(lane == i1, neg, el)
    v2 = jnp.max(el2, axis=-1, keepdims=True)
    i2 = jnp.min(jnp.where(el2 == v2, lane, big), axis=-1, keepdims=True)
    e2 = jnp.exp(v2 - v1)
    w1 = g_w / (1.0 + e2)
    w2 = g_w * e2 / (1.0 + e2)
    return jnp.where(lane == i1, w1, 0.0) + jnp.where(lane == i2, w2, 0.0)


def _outproj_body(x_ref, oda_ref, ort_ref, wo_ref, gf_ref, wr_hi_ref, wr_lo_ref, br_ref,
                  x1_ref, hf_ref, gate_ref):
    x1 = (x_ref[...]
          + jnp.dot(oda_ref[...], wo_ref[:DA_WIDTH, :], preferred_element_type=F32)
          + jnp.dot(ort_ref[...], wo_ref[DA_WIDTH:, :], preferred_element_type=F32))
    x1_ref[...] = x1
    hf = _rms(x1) * gf_ref[...]
    hf_hi = hf.astype(BF16)
    hf_ref[...] = hf
    hf_lo = (hf - hf_hi.astype(F32)).astype(BF16)
    wr_hi = wr_hi_ref[...]
    logits = (jnp.dot(hf_hi, wr_hi, preferred_element_type=F32)
              + jnp.dot(hf_lo, wr_hi, preferred_element_type=F32)
              + jnp.dot(hf_hi, wr_lo_ref[...], preferred_element_type=F32)
              + br_ref[...])
    gate_ref[...] = _route(logits)


def _outproj(x, o_da, o_ret, w_out, ffn_g, wr_hi, wr_lo, br, tm):
    t, d = x.shape
    row = lambda i: (i, 0)
    const = lambda i: (0, 0)
    return pl.pallas_call(
        _outproj_body,
        out_shape=(jax.ShapeDtypeStruct((t, d), F32), jax.ShapeDtypeStruct((t, d), F32),
                   jax.ShapeDtypeStruct((t, LANES), F32)),
        grid=(t // tm,),
        in_specs=[pl.BlockSpec((tm, d), row),
                  pl.BlockSpec((tm, DA_WIDTH), row),
                  pl.BlockSpec((tm, RET_WIDTH), row),
                  pl.BlockSpec((d, d), const),
                  pl.BlockSpec((1, d), const),
                  pl.BlockSpec((d, LANES), const),
                  pl.BlockSpec((d, LANES), const),
                  pl.BlockSpec((1, LANES), const)],
        out_specs=(pl.BlockSpec((tm, d), row), pl.BlockSpec((tm, d), row),
                   pl.BlockSpec((tm, LANES), row)),
        compiler_params=_params("parallel"),
        name="out_proj_router",
    )(x, o_da, o_ret, w_out, ffn_g.reshape(1, d), wr_hi, wr_lo, br)


MOE_TILE = 256


def _row_copy(src_hbm, row, dst, dst_row, sem):
    return pltpu.make_async_copy(src_hbm.at[pl.ds(row, 1), :], dst.at[pl.ds(dst_row, 1), :], sem)


def _moe_expert_body(texp_ref, nused_ref, src_first, src_next, hf_hbm, wg_ref, wu_ref, wd_ref,
                     ys_ref, xbuf, sem):
    j = pl.program_id(0)
    slot = lax.rem(j, 2)
    nused = nused_ref[0]

    def gather(idx_ref, to_slot):
        def one(r, carry):
            c = _row_copy(hf_hbm, idx_ref[0, 0, r] if idx_ref is not None else 0,
                          xbuf.at[to_slot], r, sem.at[to_slot])
            c.start() if idx_ref is not None else c.wait()
            return carry
        lax.fori_loop(0, MOE_TILE, one, 0, unroll=8)

    @pl.when(j == 0)
    def _():
        gather(src_first, slot)

    @pl.when(j + 1 < nused)
    def _():
        gather(src_next, 1 - slot)

    @pl.when(j < nused)
    def _():
        gather(None, slot)
        x = xbuf[slot].astype(BF16)
        a = jnp.dot(x, wg_ref[0], preferred_element_type=F32)
        u = jnp.dot(x, wu_ref[0], preferred_element_type=F32)
        act = (a * _sigmoid(a)) * u
        ys_ref[...] = jnp.dot(act.astype(BF16), wd_ref[0], preferred_element_type=F32)

    @pl.when(j >= nused)
    def _():
        ys_ref[...] = jnp.zeros(ys_ref.shape, F32)


def _moe_combine_body(pos_first, pos_next, x1_ref, w_ref, ys_hbm, y_ref, gbuf, sem):
    i = pl.program_id(0)
    slot = lax.rem(i, 2)
    tc = x1_ref.shape[0]

    def gather(idx_ref, to_slot):
        def one(r, carry):
            for k in range(2):
                c = _row_copy(ys_hbm, idx_ref[0, 0, 2 * r + k] if idx_ref is not None else 0,
                              gbuf.at[to_slot, k], r, sem.at[to_slot])
                c.start() if idx_ref is not None else c.wait()
            return carry
        lax.fori_loop(0, tc, one, 0, unroll=4)

    @pl.when(i == 0)
    def _():
        gather(pos_first, slot)

    @pl.when(i + 1 < pl.num_programs(0))
    def _():
        gather(pos_next, 1 - slot)

    gather(None, slot)
    w = w_ref[...]
    y_ref[...] = x1_ref[...] + w[:, 0:1] * gbuf[slot, 0] + w[:, 1:2] * gbuf[slot, 1]


def _moe_dense_body(hf_ref, gate_ref, x1_ref, wg_ref, wu_ref, wd_ref, y_ref):
    e = pl.program_id(1)

    @pl.when(e == 0)
    def _():
        y_ref[...] = x1_ref[...]

    h = hf_ref[...].astype(BF16)
    a = jnp.dot(h, wg_ref[0], preferred_element_type=F32)
    u = jnp.dot(h, wu_ref[0], preferred_element_type=F32)
    gate = gate_ref[...]
    lane = lax.broadcasted_iota(jnp.int32, gate.shape, 1)
    ge = jnp.sum(jnp.where(lane == e, gate, 0.0), axis=-1, keepdims=True)
    act = (a * _sigmoid(a)) * u * ge
    y_ref[...] += jnp.dot(act.astype(BF16), wd_ref[0], preferred_element_type=F32)


def _moe_dense(hf, gate, x1, w_gate, w_up, w_down, tm):
    t, d = hf.shape
    row = lambda i, e: (i, 0)
    return pl.pallas_call(
        _moe_dense_body,
        out_shape=jax.ShapeDtypeStruct((t, d), F32),
        grid=(t // tm, N_EXPERTS),
        in_specs=[pl.BlockSpec((tm, d), row),
                  pl.BlockSpec((tm, LANES), row),
                  pl.BlockSpec((tm, d), row),
                  pl.BlockSpec((1, d, D_FF), lambda i, e: (e, 0, 0)),
                  pl.BlockSpec((1, d, D_FF), lambda i, e: (e, 0, 0)),
                  pl.BlockSpec((1, D_FF, d), lambda i, e: (e, 0, 0))],
        out_specs=pl.BlockSpec((tm, d), row),
        compiler_params=_params("parallel", "arbitrary"),
        name="moe_dense",
    )(hf, gate, x1, w_gate, w_up, w_down)


def _moe(hf, gate, x1, w_gate, w_up, w_down, tm):
    t, d = hf.shape
    n_pair = 2 * t
    if n_pair < N_EXPERTS * MOE_TILE:
        return _moe_dense(hf, gate, x1, w_gate, w_up, w_down, tm)
    w_pair, e_pair = lax.top_k(gate[:, :N_EXPERTS], 2)
    e_flat = e_pair.reshape(-1).astype(jnp.int32)
    n_pad = n_pair + N_EXPERTS * MOE_TILE
    n_tiles = n_pad // MOE_TILE
    onehot = (e_flat[:, None] == jnp.arange(N_EXPERTS, dtype=jnp.int32)[None, :]).astype(jnp.int32)
    csum = jnp.cumsum(onehot, axis=0)
    counts = csum[-1]
    rank = jnp.take_along_axis(csum, e_flat[:, None], axis=1)[:, 0] - 1
    padded = ((counts + MOE_TILE - 1) // MOE_TILE) * MOE_TILE
    ends = jnp.cumsum(padded)
    starts = ends - padded
    pos = (starts[e_flat] + rank).astype(jnp.int32)
    tile_start = jnp.arange(n_tiles, dtype=jnp.int32) * MOE_TILE
    tile_expert = jnp.minimum(jnp.searchsorted(ends, tile_start, side="right"),
                              N_EXPERTS - 1).astype(jnp.int32)
    n_used = (ends[-1] // MOE_TILE).astype(jnp.int32).reshape(1)
    order = jnp.argsort(e_flat, stable=True).astype(jnp.int32)
    slot_e = jnp.repeat(tile_expert, MOE_TILE)
    slot_r = jnp.arange(n_pad, dtype=jnp.int32) - starts[slot_e]
    sorted_idx = jnp.clip((jnp.cumsum(counts) - counts)[slot_e] + slot_r, 0, n_pair - 1)
    src_tok = jnp.where(slot_r < counts[slot_e], order[sorted_idx] // 2, 0).astype(jnp.int32)

    def idx_specs(width, n_blocks):
        first = lambda *a: (0, 0, 0)
        nxt = lambda *a: (jnp.minimum(a[0] + 1, n_blocks - 1), 0, 0)
        return [pl.BlockSpec((1, 1, width), first, memory_space=pltpu.SMEM),
                pl.BlockSpec((1, 1, width), nxt, memory_space=pltpu.SMEM)]

    src3 = src_tok.reshape(n_tiles, 1, MOE_TILE)
    wspec = lambda shape: pl.BlockSpec(shape, lambda j, te, nu: (te[j], 0, 0))
    ys = pl.pallas_call(
        _moe_expert_body,
        out_shape=jax.ShapeDtypeStruct((n_pad, d), F32),
        grid_spec=pltpu.PrefetchScalarGridSpec(
            num_scalar_prefetch=2,
            grid=(n_tiles,),
            in_specs=idx_specs(MOE_TILE, n_tiles) + [
                pl.BlockSpec(memory_space=pl.ANY),
                wspec((1, d, D_FF)), wspec((1, d, D_FF)), wspec((1, D_FF, d))],
            out_specs=pl.BlockSpec((MOE_TILE, d), lambda j, te, nu: (j, 0)),
            scratch_shapes=[pltpu.VMEM((2, MOE_TILE, d), F32), pltpu.SemaphoreType.DMA((2,))]),
        compiler_params=_params("arbitrary"),
        name="moe_experts",
    )(tile_expert, n_used, src3, src3, hf, w_gate, w_up, w_down)

    tc = min(tm, MOE_TILE)
    pos3 = pos.reshape(t // tc, 1, 2 * tc)
    return pl.pallas_call(
        _moe_combine_body,
        out_shape=jax.ShapeDtypeStruct((t, d), F32),
        grid=(t // tc,),
        in_specs=idx_specs(2 * tc, t // tc) + [
            pl.BlockSpec((tc, d), lambda i: (i, 0)),
            pl.BlockSpec((tc, 2), lambda i: (i, 0)),
            pl.BlockSpec(memory_space=pl.ANY)],
        out_specs=pl.BlockSpec((tc, d), lambda i: (i, 0)),
        scratch_shapes=[pltpu.VMEM((2, 2, tc, d), F32), pltpu.SemaphoreType.DMA((2,))],
        compiler_params=_params("arbitrary"),
        name="moe_combine",
    )(pos3, pos3, x1, w_pair, ys)


def _split_hi_lo(w):
    hi = w.astype(BF16)
    return hi, (w - hi.astype(F32)).astype(BF16)


def kernel(x_prompt, x_sample, cache_k_diff, cache_v_diff, state_retention, attn_norm_g, w_in, da_q_norm_g, da_k_norm_g, da_lambda_q1, da_lambda_k1, da_lambda_q2, da_lambda_k2, da_out_norm_g, ret_out_norm_g, w_out, ffn_norm_g, w_group, b_group, w_expert, b_expert, w_gate, w_up, w_down):
    assert w_in.shape[0] == 1, "single-layer model"
    bp, seq, d = x_prompt.shape
    bd, t_dec, _ = x_sample.shape
    past = cache_k_diff.shape[2]
    assert bp == 1

    w_in_b = w_in[0].astype(BF16)
    w_out_b = w_out[0].astype(BF16)
    w_gate_b = w_gate[0].astype(BF16)
    w_up_b = w_up[0].astype(BF16)
    w_down_b = w_down[0].astype(BF16)
    w_router = jnp.concatenate([w_expert[0].reshape(d, N_EXPERTS), w_group[0]], axis=1)
    w_router = jnp.pad(w_router, ((0, 0), (0, LANES - w_router.shape[1])))
    wr_hi, wr_lo = _split_hi_lo(w_router)
    b_router = jnp.concatenate([b_expert[0].reshape(N_EXPERTS), b_group[0]])
    b_router = jnp.pad(b_router, (0, LANES - b_router.shape[0])).reshape(1, LANES)

    lams = _lambda_args(da_lambda_q1[0], da_lambda_k1[0], da_lambda_q2[0], da_lambda_k2[0])

    def layer(x, cos, sin, attend, want_vt, ret_seq, ret_chunk, s0, tm):
        h = _rmsnorm(x, attn_norm_g[0], tm)
        q_da, k_da, k_da_b, v_da, v_da_b, v_da_t, q_r, k_r, v_r, g_r = _project(
            h, w_in_b, da_q_norm_g[0], da_k_norm_g[0], cos, sin, tm, want_vt)
        o_da = attend(q_da, k_da_b, v_da_b, v_da_t)
        o_ret, s_new = _retention(q_r, k_r, v_r, g_r, s0, ret_out_norm_g[0], ret_seq, ret_chunk)
        x1, hf, gate = _outproj(x, o_da, o_ret, w_out_b, ffn_norm_g[0], wr_hi, wr_lo, b_router,
                                min(tm, 256))
        y = _moe(hf, gate, x1, w_gate_b, w_up_b, w_down_b, tm)
        return y, k_da, v_da, s_new

    cos_p, sin_p = _rope_tables(seq, 0)
    y_p, k_p, v_p, s_p = layer(
        x_prompt.reshape(seq, d), cos_p, sin_p,
        lambda q, k, v, vt: _da_prompt(q, k, vt[0], lams, da_out_norm_g[0], 512),
        True, seq, 256, jnp.zeros((bp, RET_HEADS, RET_DK, RET_DV), F32), 512)

    cos_s, sin_s = _rope_tables(t_dec, past)
    cos_s = jnp.tile(cos_s, (bd, 1))
    sin_s = jnp.tile(sin_s, (bd, 1))
    y_s, k_s, v_s, s_s = layer(
        x_sample.reshape(bd * t_dec, d), cos_s, sin_s,
        lambda q, k, v, vt: _da_sample(q, k, v, cache_k_diff[0], cache_v_diff[0], lams,
                                       da_out_norm_g[0], t_dec, 1024),
        False, t_dec, t_dec, state_retention[0], bd * t_dec)

    return (y_p.reshape(bp, seq, d),
            y_s.reshape(bd, t_dec, d),
            k_p.reshape(1, bp, seq, DA_HEADS, 2 * DA_DH),
            v_p.reshape(1, bp, seq, DA_HEADS, DA_DV),
            s_p.reshape(1, bp, RET_HEADS, RET_DK, RET_DV),
            k_s.reshape(1, bd, t_dec, DA_HEADS, 2 * DA_DH),
            v_s.reshape(1, bd, t_dec, DA_HEADS, DA_DV),
            s_s.reshape(1, bd, RET_HEADS, RET_DK, RET_DV))
```

```python
import functools
import math

import jax
import jax.numpy as jnp
from jax import lax
from jax.experimental import pallas as pl
from jax.experimental.pallas import tpu as pltpu

D_MODEL = 2048
CHUNK = 64
DA_HEADS = 4
DA_DH = 128
DA_DV = 2 * DA_DH
DA_WIDTH = DA_HEADS * DA_DV
RET_HEADS = 4
RET_DK = 256
RET_DV = 256
RET_WIDTH = RET_HEADS * RET_DV
IN_GROUP = 1024
N_GROUPS = 4
EXP_PER_GROUP = 8
N_EXPERTS = N_GROUPS * EXP_PER_GROUP
D_FF = D_MODEL // 8
EPS = 1e-6
NEG_INF = -1e30
ROPE_BASE = 10000.0
LAM_INIT = 0.8 - 0.6 * math.exp(-0.3 * 0)

LANES = 128
VMEM_LIMIT = 48 * 1024 * 1024

F32 = jnp.float32
BF16 = jnp.bfloat16


def _params(*sem):
    return pltpu.CompilerParams(dimension_semantics=sem, vmem_limit_bytes=VMEM_LIMIT)


def _sigmoid(x):
    return 1.0 / (1.0 + jnp.exp(-x))


def _rms(x):
    return x * lax.rsqrt(jnp.mean(x * x, axis=-1, keepdims=True) + EPS)


def _rmsnorm_body(x_ref, g_ref, o_ref):
    o_ref[...] = (_rms(x_ref[...]) * g_ref[...]).astype(o_ref.dtype)


def _rmsnorm(x, g, tm):
    t, d = x.shape
    return pl.pallas_call(
        _rmsnorm_body,
        out_shape=jax.ShapeDtypeStruct((t, d), BF16),
        grid=(t // tm,),
        in_specs=[pl.BlockSpec((tm, d), lambda i: (i, 0)),
                  pl.BlockSpec((1, d), lambda i: (0, 0))],
        out_specs=pl.BlockSpec((tm, d), lambda i: (i, 0)),
        compiler_params=_params("parallel"),
        name="attn_norm",
    )(x, g.reshape(1, d))


def _rope_table_body(pos0, tr, invf_ref, cos_ref, sin_ref):
    row = lax.broadcasted_iota(jnp.int32, (tr, LANES), 0) + (pl.program_id(0) * tr + pos0)
    ang = row.astype(F32) * invf_ref[...]
    cos_ref[...] = jnp.cos(ang)
    sin_ref[...] = jnp.sin(ang)


def _rope_tables(n_pos, pos0):
    half = RET_DK // 2
    inv_freq = (ROPE_BASE ** (-jnp.arange(half, dtype=F32) / half)).reshape(1, half)
    tr = min(n_pos, 512)
    spec = pl.BlockSpec((tr, half), lambda i: (i, 0))
    return pl.pallas_call(
        functools.partial(_rope_table_body, pos0, tr),
        out_shape=(jax.ShapeDtypeStruct((n_pos, half), F32),) * 2,
        grid=(n_pos // tr,),
        in_specs=[pl.BlockSpec((1, half), lambda i: (0, 0))],
        out_specs=(spec, spec),
        compiler_params=_params("parallel"),
        name="rope_tables",
    )(inv_freq)


def _store_cols(o, sl, val):
    if len(o.shape) == 2:
        o[:, sl] = val.astype(o.dtype)
    else:
        hd, off = divmod(sl.start, DA_DV)
        o[:, hd, off:off + (sl.stop - sl.start)] = val.astype(o.dtype)


def _store_all(z, outs):
    for o in outs:
        for hd in range(DA_HEADS):
            sl = slice(hd * DA_DV, (hd + 1) * DA_DV)
            _store_cols(o, sl, z[:, sl])


def _proj_plain_body(h_ref, w_ref, *outs):
    z = jnp.dot(h_ref[...], w_ref[...], preferred_element_type=F32)
    _store_all(z, outs)


def _proj_plain_t_body(h_ref, w_ref, *outs):
    z = jnp.dot(h_ref[...], w_ref[...], preferred_element_type=F32)
    _store_all(z, outs[:-1])
    outs[-1][...] = z.T.astype(outs[-1].dtype)


def _proj_qknorm_body(scale, h_ref, w_ref, g_ref, *outs):
    z = jnp.dot(h_ref[...], w_ref[...], preferred_element_type=F32)
    g = g_ref[...]
    for c in range(IN_GROUP // DA_DH):
        sl = slice(c * DA_DH, (c + 1) * DA_DH)
        zc = _rms(z[:, sl]) * g
        for o in outs:
            _store_cols(o, sl, zc * scale if o.dtype == BF16 else zc)


def _proj_rotary_body(scale, h_ref, w_ref, cos_ref, sin_ref, o_ref):
    z = jnp.dot(h_ref[...], w_ref[...], preferred_element_type=F32) * scale
    cos = cos_ref[...]
    sin = sin_ref[...]
    half = RET_DK // 2
    for hd in range(RET_HEADS):
        x1 = z[:, hd * RET_DK: hd * RET_DK + half]
        x2 = z[:, hd * RET_DK + half: (hd + 1) * RET_DK]
        o_ref[:, hd * RET_DK: hd * RET_DK + half] = (x1 * cos - x2 * sin).astype(o_ref.dtype)
        o_ref[:, hd * RET_DK + half: (hd + 1) * RET_DK] = (x1 * sin + x2 * cos).astype(o_ref.dtype)


def _proj(h, w_in, group, body, extra, extra_specs, out_dtypes, tm, name, per_head=False,
          transposed_out=False):
    t, d = h.shape
    row = lambda i: (i, 0)
    split = lambda dt: per_head and dt == F32
    shape = lambda dt: (t, DA_HEADS, DA_DV) if split(dt) else (t, IN_GROUP)
    block = lambda dt: (pl.BlockSpec((tm, DA_HEADS, DA_DV), lambda i: (i, 0, 0)) if split(dt)
                        else pl.BlockSpec((tm, IN_GROUP), row))
    outs = tuple(jax.ShapeDtypeStruct(shape(dt), dt) for dt in out_dtypes)
    out_specs = tuple(block(dt) for dt in out_dtypes)
    if transposed_out:
        outs += (jax.ShapeDtypeStruct((IN_GROUP, t), BF16),)
        out_specs += (pl.BlockSpec((IN_GROUP, tm), lambda i: (0, i)),)
    res = pl.pallas_call(
        body,
        out_shape=outs,
        grid=(t // tm,),
        in_specs=[pl.BlockSpec((tm, d), row),
                  pl.BlockSpec((d, IN_GROUP), lambda i: (0, group))] + extra_specs,
        out_specs=out_specs,
        compiler_params=_params("parallel"),
        name=name,
    )(h, w_in, *extra)
    return res


def _project(h, w_in, q_norm_g, k_norm_g, cos, sin, tm, want_vt):
    half = RET_DK // 2
    row = lambda i: (i, 0)
    gspec = [pl.BlockSpec((1, DA_DH), lambda i: (0, 0))]
    rspec = [pl.BlockSpec((tm, half), row), pl.BlockSpec((tm, half), row)]
    (q_da,) = _proj(h, w_in, 0, functools.partial(_proj_qknorm_body, DA_DH ** -0.5),
                    [q_norm_g.reshape(1, DA_DH)], gspec, [BF16], tm, "proj_q_da")
    k_da, k_da_b = _proj(h, w_in, 1, functools.partial(_proj_qknorm_body, 1.0),
                         [k_norm_g.reshape(1, DA_DH)], gspec, [F32, BF16], tm, "proj_k_da",
                         per_head=True)
    v_da, v_da_b, *v_da_t = _proj(h, w_in, 2, _proj_plain_t_body if want_vt else _proj_plain_body,
                                  [], [], [F32, BF16], tm, "proj_v_da", per_head=True,
                                  transposed_out=want_vt)
    (q_r,) = _proj(h, w_in, 3, functools.partial(_proj_rotary_body, 1.0),
                   [cos, sin], rspec, [BF16], tm, "proj_q_ret")
    (k_r,) = _proj(h, w_in, 4, functools.partial(_proj_rotary_body, RET_DK ** -0.5),
                   [cos, sin], rspec, [BF16], tm, "proj_k_ret")
    (v_r,) = _proj(h, w_in, 5, _proj_plain_body, [], [], [BF16], tm, "proj_v_ret")
    (g_r,) = _proj(h, w_in, 6, _proj_plain_body, [], [], [F32], tm, "proj_g_ret")
    return q_da, k_da, k_da_b, v_da, v_da_b, v_da_t, q_r, k_r, v_r, g_r


def _diff_lambda(lq1, lk1, lq2, lk2):
    s1 = jnp.sum(lq1[...] * lk1[...], axis=-1, keepdims=True)
    s2 = jnp.sum(lq2[...] * lk2[...], axis=-1, keepdims=True)
    return jnp.exp(s1) - jnp.exp(s2) + LAM_INIT


def _softmax_step(q, k, v, m_ref, l_ref, acc_ref, idx, mask):
    s = lax.dot_general(q, k, (((1,), (1,)), ((), ())), preferred_element_type=F32)
    if mask is not None:
        s = jnp.where(mask, s, NEG_INF)
    m_prev = m_ref[idx]
    m_new = jnp.maximum(m_prev, jnp.max(s, axis=-1, keepdims=True))
    alpha = jnp.exp(m_prev - m_new)
    p = jnp.exp(s - m_new)
    l_ref[idx] = alpha * l_ref[idx] + jnp.sum(p, axis=-1, keepdims=True)
    acc_ref[idx] = alpha * acc_ref[idx] + jnp.dot(p.astype(BF16), v, preferred_element_type=F32)
    m_ref[idx] = m_new


def _softmax_init(m_ref, l_ref, acc_ref):
    m_ref[...] = jnp.full(m_ref.shape, NEG_INF, F32)
    l_ref[...] = jnp.zeros(l_ref.shape, F32)
    acc_ref[...] = jnp.zeros(acc_ref.shape, F32)


def _diff_combine(lam, gain, l_ref, acc_ref, i0, i1):
    o = acc_ref[i0] / l_ref[i0] - lam * (acc_ref[i1] / l_ref[i1])
    return _rms(o) * gain * (1.0 - LAM_INIT)


def _da_prompt_body(tq, lq1, lk1, lq2, lk2, gain_ref, q_ref, k_ref, vt_ref, o_ref,
                    m_ref, l_ref, acc_ref):
    qi = pl.program_id(1)
    _softmax_init(m_ref, l_ref, acc_ref)

    def block(start, mask):
        k = k_ref[pl.ds(start, tq), :]
        vt = vt_ref[:, pl.ds(start, tq)]
        sts = []
        for mp in range(2):
            sl = slice(mp * DA_DH, (mp + 1) * DA_DH)
            sts.append(lax.dot_general(k[:, sl], q_ref[:, sl], (((1,), (1,)), ((), ())),
                                       preferred_element_type=F32))
        pts, alphas = [], []
        for mp in range(2):
            st = sts[mp]
            if mask is not None:
                st = jnp.where(mask, st, NEG_INF)
            m_prev = m_ref[mp]
            m_new = jnp.maximum(m_prev, jnp.max(st, axis=0, keepdims=True))
            alpha = jnp.exp(m_prev - m_new)
            pt = jnp.exp(st - m_new)
            l_ref[mp] = alpha * l_ref[mp] + jnp.sum(pt, axis=0, keepdims=True)
            m_ref[mp] = m_new
            pts.append(pt.astype(BF16))
            alphas.append(alpha)
        for mp in range(2):
            acc_ref[mp] = alphas[mp] * acc_ref[mp] + jnp.dot(vt, pts[mp],
                                                             preferred_element_type=F32)

    def full_block(kb, carry):
        block(pl.multiple_of(kb * tq, tq), None)
        return carry

    lax.fori_loop(0, qi, full_block, 0)
    kc = lax.shift_right_logical(lax.broadcasted_iota(jnp.int32, (tq, tq), 0), 6)
    qc = lax.shift_right_logical(lax.broadcasted_iota(jnp.int32, (tq, tq), 1), 6)
    block(pl.multiple_of(qi * tq, tq), kc <= qc)

    lam = _diff_lambda(lq1, lk1, lq2, lk2)
    ot = acc_ref[0] / l_ref[0] - lam * (acc_ref[1] / l_ref[1])
    ot = ot * lax.rsqrt(jnp.mean(ot * ot, axis=0, keepdims=True) + EPS)
    ot = ot * gain_ref[...] * (1.0 - LAM_INIT)
    o_ref[...] = ot.T.astype(o_ref.dtype)


def _lambda_args(lq1, lk1, lq2, lk2):
    return [a.reshape(1, DA_DH) for a in (lq1, lk1, lq2, lk2)]


def _da_prompt(q, k, vt, lams, out_gain, tq):
    s = q.shape[0]
    assert CHUNK == 64 and tq % CHUNK == 0 and s % tq == 0
    const = lambda h, i: (0, 0)
    vec = pl.BlockSpec((1, DA_DH), const)
    return pl.pallas_call(
        functools.partial(_da_prompt_body, tq),
        out_shape=jax.ShapeDtypeStruct((s, DA_WIDTH), BF16),
        grid=(DA_HEADS, s // tq),
        in_specs=[vec, vec, vec, vec,
                  pl.BlockSpec((DA_DV, 1), const),
                  pl.BlockSpec((tq, DA_DV), lambda h, i: (i, h)),
                  pl.BlockSpec((s, DA_DV), lambda h, i: (0, h)),
                  pl.BlockSpec((DA_DV, s), lambda h, i: (h, 0))],
        out_specs=pl.BlockSpec((tq, DA_DV), lambda h, i: (i, h)),
        scratch_shapes=[pltpu.VMEM((2, 1, tq), F32), pltpu.VMEM((2, 1, tq), F32),
                        pltpu.VMEM((2, DA_DV, tq), F32)],
        compiler_params=_params("parallel", "parallel"),
        name="diff_attn_prompt",
    )(*lams, out_gain.reshape(DA_DV, 1), q, k, vt)


def _da_sample_body(tk, lq1, lk1, lq2, lk2, gain_ref, q_ref, kn_ref, vn_ref, ck_hbm, cv_hbm, o_ref,
                    kbuf, vbuf, sem, m_ref, l_ref, acc_ref):
    n = pl.program_id(0) * DA_HEADS + pl.program_id(1)
    total = pl.num_programs(0) * DA_HEADS
    slot = lax.rem(n, 2)

    def copies(step, to_slot):
        stream = lax.div(step, DA_HEADS)
        head = lax.rem(step, DA_HEADS)
        return (pltpu.make_async_copy(ck_hbm.at[stream, :, head, :], kbuf.at[to_slot], sem.at[0, to_slot]),
                pltpu.make_async_copy(cv_hbm.at[stream, :, head, :], vbuf.at[to_slot], sem.at[1, to_slot]))

    @pl.when(n == 0)
    def _():
        for c in copies(n, slot):
            c.start()

    @pl.when(n + 1 < total)
    def _():
        for c in copies(n + 1, 1 - slot):
            c.start()

    def attend(k, v):
        for mp in range(2):
            sl = slice(mp * DA_DH, (mp + 1) * DA_DH)
            _softmax_step(q_ref[:, sl], k[:, sl], v, m_ref, l_ref, acc_ref, mp, None)

    _softmax_init(m_ref, l_ref, acc_ref)
    attend(kn_ref[...], vn_ref[...])
    for c in copies(n, slot):
        c.wait()
    for j in range(kbuf.shape[1] // tk):
        attend(kbuf[slot, j * tk:(j + 1) * tk, :].astype(BF16),
               vbuf[slot, j * tk:(j + 1) * tk, :].astype(BF16))
    lam = _diff_lambda(lq1, lk1, lq2, lk2)
    o_ref[...] = _diff_combine(lam, gain_ref[...], l_ref, acc_ref, 0, 1).astype(o_ref.dtype)


def _da_sample(q, k_new, v_new, cache_k, cache_v, lams, out_gain, t, tk):
    b, p = cache_k.shape[:2]
    const = lambda i, h: (0, 0)
    vec = pl.BlockSpec((1, DA_DH), const)
    tok = pl.BlockSpec((t, DA_DV), lambda i, h: (i, h))
    hbm = pl.BlockSpec(memory_space=pl.ANY)
    return pl.pallas_call(
        functools.partial(_da_sample_body, tk),
        out_shape=jax.ShapeDtypeStruct((b * t, DA_WIDTH), BF16),
        grid=(b, DA_HEADS),
        in_specs=[vec, vec, vec, vec, pl.BlockSpec((1, DA_DV), const), tok, tok, tok, hbm, hbm],
        out_specs=tok,
        scratch_shapes=[pltpu.VMEM((2, p, DA_DV), F32), pltpu.VMEM((2, p, DA_DV), F32),
                        pltpu.SemaphoreType.DMA((2, 2)),
                        pltpu.VMEM((2, t, 1), F32), pltpu.VMEM((2, t, 1), F32),
                        pltpu.VMEM((2, t, DA_DV), F32)],
        compiler_params=_params("arbitrary", "arbitrary"),
        name="diff_attn_sample",
    )(*lams, out_gain.reshape(1, DA_DV), q, k_new, v_new, cache_k, cache_v)


def _retention_body(lc, lg_ref, gain_ref, q_ref, k_ref, v_ref, g_ref, s0_ref, o_ref, s_out_ref,
                    state_ref):
    c = pl.program_id(2)

    @pl.when(c == 0)
    def _():
        state_ref[...] = s0_ref[0, 0]

    lg = lg_ref[0][:, :1]
    q = q_ref[...]
    k = k_ref[...]
    v = v_ref[...]
    rel = (lax.broadcasted_iota(jnp.int32, (lc, lc), 0)
           - lax.broadcasted_iota(jnp.int32, (lc, lc), 1)).astype(F32)
    decay = jnp.where(rel >= 0, jnp.exp(lg * jnp.maximum(rel, 0.0)), 0.0)
    scores = lax.dot_general(q, k, (((1,), (1,)), ((), ())), preferred_element_type=F32) * decay
    idx = lax.broadcasted_iota(jnp.int32, (lc, 1), 0).astype(F32)
    state = state_ref[...]
    o = jnp.dot(scores.astype(BF16), v, preferred_element_type=F32)
    o = o + jnp.dot(q, state.astype(BF16), preferred_element_type=F32) * jnp.exp(lg * (idx + 1.0))
    kw = k.astype(F32) * jnp.exp(lg * (lc - 1.0 - idx))
    s_new = state * jnp.exp(lg * lc) + jnp.dot(kw.T.astype(BF16), v, preferred_element_type=F32)
    state_ref[...] = s_new

    g = g_ref[...]
    o_ref[...] = (_rms(o) * gain_ref[...] * (g * _sigmoid(g))).astype(o_ref.dtype)

    @pl.when(c == pl.num_programs(2) - 1)
    def _():
        s_out_ref[0, 0] = s_new


def _retention(q, k, v, g, s0, out_gain, seq, lc):
    b = s0.shape[0]
    nc = seq // lc
    log_gamma = jnp.log(1.0 - 2.0 ** (-5.0 - jnp.arange(RET_HEADS, dtype=F32)))
    lg = jnp.broadcast_to(log_gamma.reshape(RET_HEADS, 1, 1), (RET_HEADS, 1, LANES))
    tok = pl.BlockSpec((lc, RET_DV), lambda bi, h, c: (bi * nc + c, h))
    st = pl.BlockSpec((1, 1, RET_DK, RET_DV), lambda bi, h, c: (bi, h, 0, 0))
    return pl.pallas_call(
        functools.partial(_retention_body, lc),
        out_shape=(jax.ShapeDtypeStruct((b * seq, RET_WIDTH), BF16),
                   jax.ShapeDtypeStruct(s0.shape, F32)),
        grid=(b, RET_HEADS, nc),
        in_specs=[pl.BlockSpec((1, 1, LANES), lambda bi, h, c: (h, 0, 0)),
                  pl.BlockSpec((1, RET_DV), lambda bi, h, c: (0, 0)),
                  tok, tok, tok, tok, st],
        out_specs=(tok, st),
        scratch_shapes=[pltpu.VMEM((RET_DK, RET_DV), F32)],
        compiler_params=_params("parallel", "parallel", "arbitrary"),
        name="retention",
    )(lg, out_gain.reshape(1, RET_DV), q, k, v, g, s0)


def _route(logits):
    lane = lax.broadcasted_iota(jnp.int32, logits.shape, 1)
    big = jnp.int32(LANES)
    neg = -jnp.inf
    gl = jnp.where((lane >= N_EXPERTS) & (lane < N_EXPERTS + N_GROUPS), logits, neg)
    g_max = jnp.max(gl, axis=-1, keepdims=True)
    g_idx = jnp.min(jnp.where(gl == g_max, lane - N_EXPERTS, big), axis=-1, keepdims=True)
    g_w = 1.0 / jnp.sum(jnp.exp(gl - g_max), axis=-1, keepdims=True)
    in_group = (lane < N_EXPERTS) & (lax.shift_right_logical(lane, 3) == g_idx)
    el = jnp.where(in_group, logits, neg)
    v1 = jnp.max(el, axis=-1, keepdims=True)
    i1 = jnp.min(jnp.where(el == v1, lane, big), axis=-1, keepdims=True)
    el2 = jnp.where(lane == i1, neg, el)
    v2 = jnp.max(el2, axis=-1, keepdims=True)
    i2 = jnp.min(jnp.where(el2 == v2, lane, big), axis=-1, keepdims=True)
    e2 = jnp.exp(v2 - v1)
    w1 = g_w / (1.0 + e2)
    w2 = g_w * e2 / (1.0 + e2)
    return jnp.where(lane == i1, w1, 0.0) + jnp.where(lane == i2, w2, 0.0)


def _outproj_body(x_ref, oda_ref, ort_ref, wo_ref, gf_ref, wr_hi_ref, wr_lo_ref, br_ref,
                  x1_ref, hf_ref, gate_ref):
    x1 = (x_ref[...]
          + jnp.dot(oda_ref[...], wo_ref[:DA_WIDTH, :], preferred_element_type=F32)
          + jnp.dot(ort_ref[...], wo_ref[DA_WIDTH:, :], preferred_element_type=F32))
    x1_ref[...] = x1
    hf = _rms(x1) * gf_ref[...]
    hf_hi = hf.astype(BF16)
    hf_ref[...] = hf
    hf_lo = (hf - hf_hi.astype(F32)).astype(BF16)
    wr_hi = wr_hi_ref[...]
    logits = (jnp.dot(hf_hi, wr_hi, preferred_element_type=F32)
              + jnp.dot(hf_lo, wr_hi, preferred_element_type=F32)
              + jnp.dot(hf_hi, wr_lo_ref[...], preferred_element_type=F32)
              + br_ref[...])
    gate_ref[...] = _route(logits)


def _outproj(x, o_da, o_ret, w_out, ffn_g, wr_hi, wr_lo, br, tm):
    t, d = x.shape
    row = lambda i: (i, 0)
    const = lambda i: (0, 0)
    return pl.pallas_call(
        _outproj_body,
        out_shape=(jax.ShapeDtypeStruct((t, d), F32), jax.ShapeDtypeStruct((t, d), F32),
                   jax.ShapeDtypeStruct((t, LANES), F32)),
        grid=(t // tm,),
        in_specs=[pl.BlockSpec((tm, d), row),
                  pl.BlockSpec((tm, DA_WIDTH), row),
                  pl.BlockSpec((tm, RET_WIDTH), row),
                  pl.BlockSpec((d, d), const),
                  pl.BlockSpec((1, d), const),
                  pl.BlockSpec((d, LANES), const),
                  pl.BlockSpec((d, LANES), const),
                  pl.BlockSpec((1, LANES), const)],
        out_specs=(pl.BlockSpec((tm, d), row), pl.BlockSpec((tm, d), row),
                   pl.BlockSpec((tm, LANES), row)),
        compiler_params=_params("parallel"),
        name="out_proj_router",
    )(x, o_da, o_ret, w_out, ffn_g.reshape(1, d), wr_hi, wr_lo, br)


MOE_TILE = 256


def _row_copy(src_hbm, row, dst, dst_row, sem):
    return pltpu.make_async_copy(src_hbm.at[pl.ds(row, 1), :], dst.at[pl.ds(dst_row, 1), :], sem)


def _moe_expert_body(texp_ref, nused_ref, src_first, src_next, hf_hbm, wg_ref, wu_ref, wd_ref,
                     ys_ref, xbuf, sem):
    j = pl.program_id(0)
    slot = lax.rem(j, 2)
    nused = nused_ref[0]

    def gather(idx_ref, to_slot):
        def one(r, carry):
            c = _row_copy(hf_hbm, idx_ref[0, 0, r] if idx_ref is not None else 0,
                          xbuf.at[to_slot], r, sem.at[to_slot])
            c.start() if idx_ref is not None else c.wait()
            return carry
        lax.fori_loop(0, MOE_TILE, one, 0, unroll=8)

    @pl.when(j == 0)
    def _():
        gather(src_first, slot)

    @pl.when(j + 1 < nused)
    def _():
        gather(src_next, 1 - slot)

    @pl.when(j < nused)
    def _():
        gather(None, slot)
        x = xbuf[slot].astype(BF16)
        a = jnp.dot(x, wg_ref[0].astype(BF16), preferred_element_type=F32)
        u = jnp.dot(x, wu_ref[0].astype(BF16), preferred_element_type=F32)
        act = (a * _sigmoid(a)) * u
        ys_ref[...] = jnp.dot(act.astype(BF16), wd_ref[0].astype(BF16), preferred_element_type=F32)

    @pl.when(j >= nused)
    def _():
        ys_ref[...] = jnp.zeros(ys_ref.shape, F32)


def _moe_combine_body(pos_first, pos_next, x1_ref, w_ref, ys_hbm, y_ref, gbuf, sem):
    i = pl.program_id(0)
    slot = lax.rem(i, 2)
    tc = x1_ref.shape[0]

    def gather(idx_ref, to_slot):
        def one(r, carry):
            for k in range(2):
                c = _row_copy(ys_hbm, idx_ref[0, 0, 2 * r + k] if idx_ref is not None else 0,
                              gbuf.at[to_slot, k], r, sem.at[to_slot])
                c.start() if idx_ref is not None else c.wait()
            return carry
        lax.fori_loop(0, tc, one, 0, unroll=4)

    @pl.when(i == 0)
    def _():
        gather(pos_first, slot)

    @pl.when(i + 1 < pl.num_programs(0))
    def _():
        gather(pos_next, 1 - slot)

    gather(None, slot)
    w = w_ref[...]
    y_ref[...] = x1_ref[...] + w[:, 0:1] * gbuf[slot, 0] + w[:, 1:2] * gbuf[slot, 1]


def _moe_dense_body(hf_ref, gate_ref, x1_ref, wg_ref, wu_ref, wd_ref, y_ref):
    e = pl.program_id(1)

    @pl.when(e == 0)
    def _():
        y_ref[...] = x1_ref[...]

    h = hf_ref[...].astype(BF16)
    a = jnp.dot(h, wg_ref[0].astype(BF16), preferred_element_type=F32)
    u = jnp.dot(h, wu_ref[0].astype(BF16), preferred_element_type=F32)
    gate = gate_ref[...]
    lane = lax.broadcasted_iota(jnp.int32, gate.shape, 1)
    ge = jnp.sum(jnp.where(lane == e, gate, 0.0), axis=-1, keepdims=True)
    act = (a * _sigmoid(a)) * u * ge
    y_ref[...] += jnp.dot(act.astype(BF16), wd_ref[0].astype(BF16), preferred_element_type=F32)


def _moe_dense(hf, gate, x1, w_gate, w_up, w_down, tm):
    t, d = hf.shape
    row = lambda i, e: (i, 0)
    return pl.pallas_call(
        _moe_dense_body,
        out_shape=jax.ShapeDtypeStruct((t, d), F32),
        grid=(t // tm, N_EXPERTS),
        in_specs=[pl.BlockSpec((tm, d), row),
                  pl.BlockSpec((tm, LANES), row),
                  pl.BlockSpec((tm, d), row),
                  pl.BlockSpec((1, d, D_FF), lambda i, e: (e, 0, 0)),
                  pl.BlockSpec((1, d, D_FF), lambda i, e: (e, 0, 0)),
                  pl.BlockSpec((1, D_FF, d), lambda i, e: (e, 0, 0))],
        out_specs=pl.BlockSpec((tm, d), row),
        compiler_params=_params("parallel", "arbitrary"),
        name="moe_dense",
    )(hf, gate, x1, w_gate, w_up, w_down)


def _moe(hf, gate, x1, w_gate, w_up, w_down, tm):
    t, d = hf.shape
    n_pair = 2 * t
    if n_pair < N_EXPERTS * MOE_TILE:
        return _moe_dense(hf, gate, x1, w_gate, w_up, w_down, tm)
    w_pair, e_pair = lax.top_k(gate[:, :N_EXPERTS], 2)
    e_flat = e_pair.reshape(-1).astype(jnp.int32)
    n_pad = n_pair + N_EXPERTS * MOE_TILE
    n_tiles = n_pad // MOE_TILE
    onehot = (e_flat[:, None] == jnp.arange(N_EXPERTS, dtype=jnp.int32)[None, :]).astype(jnp.int32)
    csum = jnp.cumsum(onehot, axis=0)
    counts = csum[-1]
    rank = jnp.take_along_axis(csum, e_flat[:, None], axis=1)[:, 0] - 1
    padded = ((counts + MOE_TILE - 1) // MOE_TILE) * MOE_TILE
    ends = jnp.cumsum(padded)
    starts = ends - padded
    pos = (starts[e_flat] + rank).astype(jnp.int32)
    tile_start = jnp.arange(n_tiles, dtype=jnp.int32) * MOE_TILE
    tile_expert = jnp.minimum(jnp.searchsorted(ends, tile_start, side="right"),
                              N_EXPERTS - 1).astype(jnp.int32)
    n_used = (ends[-1] // MOE_TILE).astype(jnp.int32).reshape(1)
    order = jnp.argsort(e_flat, stable=True).astype(jnp.int32)
    slot_e = jnp.repeat(tile_expert, MOE_TILE)
    slot_r = jnp.arange(n_pad, dtype=jnp.int32) - starts[slot_e]
    sorted_idx = jnp.clip((jnp.cumsum(counts) - counts)[slot_e] + slot_r, 0, n_pair - 1)
    src_tok = jnp.where(slot_r < counts[slot_e], order[sorted_idx] // 2, 0).astype(jnp.int32)

    def idx_specs(width, n_blocks):
        first = lambda *a: (0, 0, 0)
        nxt = lambda *a: (jnp.minimum(a[0] + 1, n_blocks - 1), 0, 0)
        return [pl.BlockSpec((1, 1, width), first, memory_space=pltpu.SMEM),
                pl.BlockSpec((1, 1, width), nxt, memory_space=pltpu.SMEM)]

    src3 = src_tok.reshape(n_tiles, 1, MOE_TILE)
    wspec = lambda shape: pl.BlockSpec(shape, lambda j, te, nu: (te[j], 0, 0))
    ys = pl.pallas_call(
        _moe_expert_body,
        out_shape=jax.ShapeDtypeStruct((n_pad, d), F32),
        grid_spec=pltpu.PrefetchScalarGridSpec(
            num_scalar_prefetch=2,
            grid=(n_tiles,),
            in_specs=idx_specs(MOE_TILE, n_tiles) + [
                pl.BlockSpec(memory_space=pl.ANY),
                wspec((1, d, D_FF)), wspec((1, d, D_FF)), wspec((1, D_FF, d))],
            out_specs=pl.BlockSpec((MOE_TILE, d), lambda j, te, nu: (j, 0)),
            scratch_shapes=[pltpu.VMEM((2, MOE_TILE, d), F32), pltpu.SemaphoreType.DMA((2,))]),
        compiler_params=_params("arbitrary"),
        name="moe_experts",
    )(tile_expert, n_used, src3, src3, hf, w_gate, w_up, w_down)

    tc = min(tm, MOE_TILE)
    pos3 = pos.reshape(t // tc, 1, 2 * tc)
    return pl.pallas_call(
        _moe_combine_body,
        out_shape=jax.ShapeDtypeStruct((t, d), F32),
        grid=(t // tc,),
        in_specs=idx_specs(2 * tc, t // tc) + [
            pl.BlockSpec((tc, d), lambda i: (i, 0)),
            pl.BlockSpec((tc, 2), lambda i: (i, 0)),
            pl.BlockSpec(memory_space=pl.ANY)],
        out_specs=pl.BlockSpec((tc, d), lambda i: (i, 0)),
        scratch_shapes=[pltpu.VMEM((2, 2, tc, d), F32), pltpu.SemaphoreType.DMA((2,))],
        compiler_params=_params("arbitrary"),
        name="moe_combine",
    )(pos3, pos3, x1, w_pair, ys)


def _split_hi_lo(w):
    hi = w.astype(BF16)
    return hi, (w - hi.astype(F32)).astype(BF16)


def kernel(x_prompt, x_sample, cache_k_diff, cache_v_diff, state_retention, attn_norm_g, w_in, da_q_norm_g, da_k_norm_g, da_lambda_q1, da_lambda_k1, da_lambda_q2, da_lambda_k2, da_out_norm_g, ret_out_norm_g, w_out, ffn_norm_g, w_group, b_group, w_expert, b_expert, w_gate, w_up, w_down):
    assert w_in.shape[0] == 1, "single-layer model"
    bp, seq, d = x_prompt.shape
    bd, t_dec, _ = x_sample.shape
    past = cache_k_diff.shape[2]
    assert bp == 1

    w_in_b = w_in[0].astype(BF16)
    w_out_b = w_out[0].astype(BF16)
    w_router = jnp.concatenate([w_expert[0].reshape(d, N_EXPERTS), w_group[0]], axis=1)
    w_router = jnp.pad(w_router, ((0, 0), (0, LANES - w_router.shape[1])))
    wr_hi, wr_lo = _split_hi_lo(w_router)
    b_router = jnp.concatenate([b_expert[0].reshape(N_EXPERTS), b_group[0]])
    b_router = jnp.pad(b_router, (0, LANES - b_router.shape[0])).reshape(1, LANES)

    lams = _lambda_args(da_lambda_q1[0], da_lambda_k1[0], da_lambda_q2[0], da_lambda_k2[0])

    def layer(x, cos, sin, attend, want_vt, ret_seq, ret_chunk, s0, tm):
        h = _rmsnorm(x, attn_norm_g[0], tm)
        q_da, k_da, k_da_b, v_da, v_da_b, v_da_t, q_r, k_r, v_r, g_r = _project(
            h, w_in_b, da_q_norm_g[0], da_k_norm_g[0], cos, sin, tm, want_vt)
        o_da = attend(q_da, k_da_b, v_da_b, v_da_t)
        o_ret, s_new = _retention(q_r, k_r, v_r, g_r, s0, ret_out_norm_g[0], ret_seq, ret_chunk)
        x1, hf, gate = _outproj(x, o_da, o_ret, w_out_b, ffn_norm_g[0], wr_hi, wr_lo, b_router,
                                min(tm, 256))
        y = _moe(hf, gate, x1, w_gate[0], w_up[0], w_down[0], tm)
        return y, k_da, v_da, s_new

    cos_p, sin_p = _rope_tables(seq, 0)
    y_p, k_p, v_p, s_p = layer(
        x_prompt.reshape(seq, d), cos_p, sin_p,
        lambda q, k, v, vt: _da_prompt(q, k, vt[0], lams, da_out_norm_g[0], 512),
        True, seq, 256, jnp.zeros((bp, RET_HEADS, RET_DK, RET_DV), F32), 512)

    cos_s, sin_s = _rope_tables(t_dec, past)
    cos_s = jnp.tile(cos_s, (bd, 1))
    sin_s = jnp.tile(sin_s, (bd, 1))
    y_s, k_s, v_s, s_s = layer(
        x_sample.reshape(bd * t_dec, d), cos_s, sin_s,
        lambda q, k, v, vt: _da_sample(q, k, v, cache_k_diff[0], cache_v_diff[0], lams,
                                       da_out_norm_g[0], t_dec, 1024),
        False, t_dec, t_dec, state_retention[0], bd * t_dec)

    return (y_p.reshape(bp, seq, d),
            y_s.reshape(bd, t_dec, d),
            k_p.reshape(1, bp, seq, DA_HEADS, 2 * DA_DH),
            v_p.reshape(1, bp, seq, DA_HEADS, DA_DV),
            s_p.reshape(1, bp, RET_HEADS, RET_DK, RET_DV),
            k_s.reshape(1, bd, t_dec, DA_HEADS, 2 * DA_DH),
            v_s.reshape(1, bd, t_dec, DA_HEADS, DA_DV),
            s_s.reshape(1, bd, RET_HEADS, RET_DK, RET_DV))
```

```python
import functools
import math

import jax
import jax.numpy as jnp
from jax import lax
from jax.experimental import pallas as pl
from jax.experimental.pallas import tpu as pltpu

D_MODEL = 2048
CHUNK = 64
DA_HEADS = 4
DA_DH = 128
DA_DV = 2 * DA_DH
DA_WIDTH = DA_HEADS * DA_DV
RET_HEADS = 4
RET_DK = 256
RET_DV = 256
RET_WIDTH = RET_HEADS * RET_DV
IN_GROUP = 1024
N_GROUPS = 4
EXP_PER_GROUP = 8
N_EXPERTS = N_GROUPS * EXP_PER_GROUP
D_FF = D_MODEL // 8
EPS = 1e-6
NEG_INF = -1e30
ROPE_BASE = 10000.0
LAM_INIT = 0.8 - 0.6 * math.exp(-0.3 * 0)

LANES = 128
VMEM_LIMIT = 48 * 1024 * 1024

F32 = jnp.float32
BF16 = jnp.bfloat16


def _params(*sem):
    return pltpu.CompilerParams(dimension_semantics=sem, vmem_limit_bytes=VMEM_LIMIT)


def _sigmoid(x):
    return 1.0 / (1.0 + jnp.exp(-x))


def _rms(x):
    return x * lax.rsqrt(jnp.mean(x * x, axis=-1, keepdims=True) + EPS)


def _rmsnorm_body(x_ref, g_ref, o_ref):
    o_ref[...] = (_rms(x_ref[...]) * g_ref[...]).astype(o_ref.dtype)


def _rmsnorm(x, g, tm):
    t, d = x.shape
    return pl.pallas_call(
        _rmsnorm_body,
        out_shape=jax.ShapeDtypeStruct((t, d), BF16),
        grid=(t // tm,),
        in_specs=[pl.BlockSpec((tm, d), lambda i: (i, 0)),
                  pl.BlockSpec((1, d), lambda i: (0, 0))],
        out_specs=pl.BlockSpec((tm, d), lambda i: (i, 0)),
        compiler_params=_params("parallel"),
        name="attn_norm",
    )(x, g.reshape(1, d))


def _rope_table_body(pos0, tr, invf_ref, cos_ref, sin_ref):
    row = lax.broadcasted_iota(jnp.int32, (tr, LANES), 0) + (pl.program_id(0) * tr + pos0)
    ang = row.astype(F32) * invf_ref[...]
    cos_ref[...] = jnp.cos(ang)
    sin_ref[...] = jnp.sin(ang)


def _rope_tables(n_pos, pos0):
    half = RET_DK // 2
    inv_freq = (ROPE_BASE ** (-jnp.arange(half, dtype=F32) / half)).reshape(1, half)
    tr = min(n_pos, 512)
    spec = pl.BlockSpec((tr, half), lambda i: (i, 0))
    return pl.pallas_call(
        functools.partial(_rope_table_body, pos0, tr),
        out_shape=(jax.ShapeDtypeStruct((n_pos, half), F32),) * 2,
        grid=(n_pos // tr,),
        in_specs=[pl.BlockSpec((1, half), lambda i: (0, 0))],
        out_specs=(spec, spec),
        compiler_params=_params("parallel"),
        name="rope_tables",
    )(inv_freq)


def _store_cols(o, sl, val):
    if len(o.shape) == 2:
        o[:, sl] = val.astype(o.dtype)
    else:
        hd, off = divmod(sl.start, DA_DV)
        o[:, hd, off:off + (sl.stop - sl.start)] = val.astype(o.dtype)


def _store_all(z, outs):
    for o in outs:
        for hd in range(DA_HEADS):
            sl = slice(hd * DA_DV, (hd + 1) * DA_DV)
            _store_cols(o, sl, z[:, sl])


def _proj_plain_body(h_ref, w_ref, *outs):
    z = jnp.dot(h_ref[...], w_ref[...], preferred_element_type=F32)
    _store_all(z, outs)


def _proj_plain_t_body(h_ref, w_ref, *outs):
    z = jnp.dot(h_ref[...], w_ref[...], preferred_element_type=F32)
    _store_all(z, outs[:-1])
    outs[-1][...] = z.T.astype(outs[-1].dtype)


def _proj_qknorm_body(scale, h_ref, w_ref, g_ref, *outs):
    z = jnp.dot(h_ref[...], w_ref[...], preferred_element_type=F32)
    g = g_ref[...]
    for c in range(IN_GROUP // DA_DH):
        sl = slice(c * DA_DH, (c + 1) * DA_DH)
        zc = _rms(z[:, sl]) * g
        for o in outs:
            _store_cols(o, sl, zc * scale if o.dtype == BF16 else zc)


def _proj_rotary_body(scale, h_ref, w_ref, cos_ref, sin_ref, o_ref):
    z = jnp.dot(h_ref[...], w_ref[...], preferred_element_type=F32) * scale
    cos = cos_ref[...]
    sin = sin_ref[...]
    half = RET_DK // 2
    for hd in range(RET_HEADS):
        x1 = z[:, hd * RET_DK: hd * RET_DK + half]
        x2 = z[:, hd * RET_DK + half: (hd + 1) * RET_DK]
        o_ref[:, hd * RET_DK: hd * RET_DK + half] = (x1 * cos - x2 * sin).astype(o_ref.dtype)
        o_ref[:, hd * RET_DK + half: (hd + 1) * RET_DK] = (x1 * sin + x2 * cos).astype(o_ref.dtype)


def _proj(h, w_in, group, body, extra, extra_specs, out_dtypes, tm, name, per_head=False,
          transposed_out=False):
    t, d = h.shape
    row = lambda i: (i, 0)
    split = lambda dt: per_head and dt == F32
    shape = lambda dt: (t, DA_HEADS, DA_DV) if split(dt) else (t, IN_GROUP)
    block = lambda dt: (pl.BlockSpec((tm, DA_HEADS, DA_DV), lambda i: (i, 0, 0)) if split(dt)
                        else pl.BlockSpec((tm, IN_GROUP), row))
    outs = tuple(jax.ShapeDtypeStruct(shape(dt), dt) for dt in out_dtypes)
    out_specs = tuple(block(dt) for dt in out_dtypes)
    if transposed_out:
        outs += (jax.ShapeDtypeStruct((IN_GROUP, t), BF16),)
        out_specs += (pl.BlockSpec((IN_GROUP, tm), lambda i: (0, i)),)
    res = pl.pallas_call(
        body,
        out_shape=outs,
        grid=(t // tm,),
        in_specs=[pl.BlockSpec((tm, d), row),
                  pl.BlockSpec((d, IN_GROUP), lambda i: (0, group))] + extra_specs,
        out_specs=out_specs,
        compiler_params=_params("parallel"),
        name=name,
    )(h, w_in, *extra)
    return res


def _project(h, w_in, q_norm_g, k_norm_g, cos, sin, tm, want_vt):
    half = RET_DK // 2
    row = lambda i: (i, 0)
    gspec = [pl.BlockSpec((1, DA_DH), lambda i: (0, 0))]
    rspec = [pl.BlockSpec((tm, half), row), pl.BlockSpec((tm, half), row)]
    (q_da,) = _proj(h, w_in, 0, functools.partial(_proj_qknorm_body, DA_DH ** -0.5),
                    [q_norm_g.reshape(1, DA_DH)], gspec, [BF16], tm, "proj_q_da")
    k_da, k_da_b = _proj(h, w_in, 1, functools.partial(_proj_qknorm_body, 1.0),
                         [k_norm_g.reshape(1, DA_DH)], gspec, [F32, BF16], tm, "proj_k_da",
                         per_head=True)
    v_da, v_da_b, *v_da_t = _proj(h, w_in, 2, _proj_plain_t_body if want_vt else _proj_plain_body,
                                  [], [], [F32, BF16], tm, "proj_v_da", per_head=True,
                                  transposed_out=want_vt)
    (q_r,) = _proj(h, w_in, 3, functools.partial(_proj_rotary_body, 1.0),
                   [cos, sin], rspec, [BF16], tm, "proj_q_ret")
    (k_r,) = _proj(h, w_in, 4, functools.partial(_proj_rotary_body, RET_DK ** -0.5),
                   [cos, sin], rspec, [BF16], tm, "proj_k_ret")
    (v_r,) = _proj(h, w_in, 5, _proj_plain_body, [], [], [BF16], tm, "proj_v_ret")
    (g_r,) = _proj(h, w_in, 6, _proj_plain_body, [], [], [F32], tm, "proj_g_ret")
    return q_da, k_da, k_da_b, v_da, v_da_b, v_da_t, q_r, k_r, v_r, g_r


def _diff_lambda(lq1, lk1, lq2, lk2):
    s1 = jnp.sum(lq1[...] * lk1[...], axis=-1, keepdims=True)
    s2 = jnp.sum(lq2[...] * lk2[...], axis=-1, keepdims=True)
    return jnp.exp(s1) - jnp.exp(s2) + LAM_INIT


def _softmax_step(q, k, v, m_ref, l_ref, acc_ref, idx, mask):
    s = lax.dot_general(q, k, (((1,), (1,)), ((), ())), preferred_element_type=F32)
    if mask is not None:
        s = jnp.where(mask, s, NEG_INF)
    m_prev = m_ref[idx]
    m_new = jnp.maximum(m_prev, jnp.max(s, axis=-1, keepdims=True))
    alpha = jnp.exp(m_prev - m_new)
    p = jnp.exp(s - m_new)
    l_ref[idx] = alpha * l_ref[idx] + jnp.sum(p, axis=-1, keepdims=True)
    acc_ref[idx] = alpha * acc_ref[idx] + jnp.dot(p.astype(BF16), v, preferred_element_type=F32)
    m_ref[idx] = m_new


def _softmax_init(m_ref, l_ref, acc_ref):
    m_ref[...] = jnp.full(m_ref.shape, NEG_INF, F32)
    l_ref[...] = jnp.zeros(l_ref.shape, F32)
    acc_ref[...] = jnp.zeros(acc_ref.shape, F32)


def _diff_combine(lam, gain, l_ref, acc_ref, i0, i1):
    o = acc_ref[i0] / l_ref[i0] - lam * (acc_ref[i1] / l_ref[i1])
    return _rms(o) * gain * (1.0 - LAM_INIT)


def _da_prompt_body(tq, lq1, lk1, lq2, lk2, gain_ref, q_ref, k_ref, vt_ref, o_ref,
                    m_ref, l_ref, acc_ref):
    qi = pl.program_id(1)
    _softmax_init(m_ref, l_ref, acc_ref)

    def block(start, mask):
        k = k_ref[pl.ds(start, tq), :]
        vt = vt_ref[:, pl.ds(start, tq)]
        sts = []
        for mp in range(2):
            sl = slice(mp * DA_DH, (mp + 1) * DA_DH)
            sts.append(lax.dot_general(k[:, sl], q_ref[:, sl], (((1,), (1,)), ((), ())),
                                       preferred_element_type=F32))
        pts, alphas = [], []
        for mp in range(2):
            st = sts[mp]
            if mask is not None:
                st = jnp.where(mask, st, NEG_INF)
            m_prev = m_ref[mp]
            m_new = jnp.maximum(m_prev, jnp.max(st, axis=0, keepdims=True))
            alpha = jnp.exp(m_prev - m_new)
            pt = jnp.exp(st - m_new)
            l_ref[mp] = alpha * l_ref[mp] + jnp.sum(pt, axis=0, keepdims=True)
            m_ref[mp] = m_new
            pts.append(pt.astype(BF16))
            alphas.append(alpha)
        for mp in range(2):
            acc_ref[mp] = alphas[mp] * acc_ref[mp] + jnp.dot(vt, pts[mp],
                                                             preferred_element_type=F32)

    def full_block(kb, carry):
        block(pl.multiple_of(kb * tq, tq), None)
        return carry

    lax.fori_loop(0, qi, full_block, 0)
    kc = lax.shift_right_logical(lax.broadcasted_iota(jnp.int32, (tq, tq), 0), 6)
    qc = lax.shift_right_logical(lax.broadcasted_iota(jnp.int32, (tq, tq), 1), 6)
    block(pl.multiple_of(qi * tq, tq), kc <= qc)

    lam = _diff_lambda(lq1, lk1, lq2, lk2)
    ot = acc_ref[0] / l_ref[0] - lam * (acc_ref[1] / l_ref[1])
    ot = ot * lax.rsqrt(jnp.mean(ot * ot, axis=0, keepdims=True) + EPS)
    ot = ot * gain_ref[...] * (1.0 - LAM_INIT)
    o_ref[...] = ot.T.astype(o_ref.dtype)


def _lambda_args(lq1, lk1, lq2, lk2):
    return [a.reshape(1, DA_DH) for a in (lq1, lk1, lq2, lk2)]


def _da_prompt(q, k, vt, lams, out_gain, tq):
    s = q.shape[0]
    assert CHUNK == 64 and tq % CHUNK == 0 and s % tq == 0
    const = lambda h, i: (0, 0)
    vec = pl.BlockSpec((1, DA_DH), const)
    return pl.pallas_call(
        functools.partial(_da_prompt_body, tq),
        out_shape=jax.ShapeDtypeStruct((s, DA_WIDTH), BF16),
        grid=(DA_HEADS, s // tq),
        in_specs=[vec, vec, vec, vec,
                  pl.BlockSpec((DA_DV, 1), const),
                  pl.BlockSpec((tq, DA_DV), lambda h, i: (i, h)),
                  pl.BlockSpec((s, DA_DV), lambda h, i: (0, h)),
                  pl.BlockSpec((DA_DV, s), lambda h, i: (h, 0))],
        out_specs=pl.BlockSpec((tq, DA_DV), lambda h, i: (i, h)),
        scratch_shapes=[pltpu.VMEM((2, 1, tq), F32), pltpu.VMEM((2, 1, tq), F32),
                        pltpu.VMEM((2, DA_DV, tq), F32)],
        compiler_params=_params("parallel", "parallel"),
        name="diff_attn_prompt",
    )(*lams, out_gain.reshape(DA_DV, 1), q, k, vt)


def _da_sample_body(tk, lq1, lk1, lq2, lk2, gain_ref, q_ref, kn_ref, vn_ref, ck_hbm, cv_hbm, o_ref,
                    kbuf, vbuf, sem, m_ref, l_ref, acc_ref):
    n = pl.program_id(0) * DA_HEADS + pl.program_id(1)
    total = pl.num_programs(0) * DA_HEADS
    slot = lax.rem(n, 2)

    def copies(step, to_slot):
        stream = lax.div(step, DA_HEADS)
        head = lax.rem(step, DA_HEADS)
        return (pltpu.make_async_copy(ck_hbm.at[stream, :, head, :], kbuf.at[to_slot], sem.at[0, to_slot]),
                pltpu.make_async_copy(cv_hbm.at[stream, :, head, :], vbuf.at[to_slot], sem.at[1, to_slot]))

    @pl.when(n == 0)
    def _():
        for c in copies(n, slot):
            c.start()

    @pl.when(n + 1 < total)
    def _():
        for c in copies(n + 1, 1 - slot):
            c.start()

    def attend(k, v):
        for mp in range(2):
            sl = slice(mp * DA_DH, (mp + 1) * DA_DH)
            _softmax_step(q_ref[:, sl], k[:, sl], v, m_ref, l_ref, acc_ref, mp, None)

    _softmax_init(m_ref, l_ref, acc_ref)
    attend(kn_ref[...], vn_ref[...])
    for c in copies(n, slot):
        c.wait()
    for j in range(kbuf.shape[1] // tk):
        attend(kbuf[slot, j * tk:(j + 1) * tk, :].astype(BF16),
               vbuf[slot, j * tk:(j + 1) * tk, :].astype(BF16))
    lam = _diff_lambda(lq1, lk1, lq2, lk2)
    o_ref[...] = _diff_combine(lam, gain_ref[...], l_ref, acc_ref, 0, 1).astype(o_ref.dtype)


def _da_sample(q, k_new, v_new, cache_k, cache_v, lams, out_gain, t, tk):
    b, p = cache_k.shape[:2]
    const = lambda i, h: (0, 0)
    vec = pl.BlockSpec((1, DA_DH), const)
    tok = pl.BlockSpec((t, DA_DV), lambda i, h: (i, h))
    hbm = pl.BlockSpec(memory_space=pl.ANY)
    return pl.pallas_call(
        functools.partial(_da_sample_body, tk),
        out_shape=jax.ShapeDtypeStruct((b * t, DA_WIDTH), BF16),
        grid=(b, DA_HEADS),
        in_specs=[vec, vec, vec, vec, pl.BlockSpec((1, DA_DV), const), tok, tok, tok, hbm, hbm],
        out_specs=tok,
        scratch_shapes=[pltpu.VMEM((2, p, DA_DV), F32), pltpu.VMEM((2, p, DA_DV), F32),
                        pltpu.SemaphoreType.DMA((2, 2)),
                        pltpu.VMEM((2, t, 1), F32), pltpu.VMEM((2, t, 1), F32),
                        pltpu.VMEM((2, t, DA_DV), F32)],
        compiler_params=_params("arbitrary", "arbitrary"),
        name="diff_attn_sample",
    )(*lams, out_gain.reshape(1, DA_DV), q, k_new, v_new, cache_k, cache_v)


def _retention_body(lc, lg_ref, gain_ref, q_ref, k_ref, v_ref, g_ref, s0_ref, o_ref, s_out_ref,
                    state_ref):
    c = pl.program_id(2)

    @pl.when(c == 0)
    def _():
        state_ref[...] = s0_ref[0, 0]

    lg = lg_ref[0][:, :1]
    q = q_ref[...]
    k = k_ref[...]
    v = v_ref[...]
    rel = (lax.broadcasted_iota(jnp.int32, (lc, lc), 0)
           - lax.broadcasted_iota(jnp.int32, (lc, lc), 1)).astype(F32)
    decay = jnp.where(rel >= 0, jnp.exp(lg * jnp.maximum(rel, 0.0)), 0.0)
    scores = lax.dot_general(q, k, (((1,), (1,)), ((), ())), preferred_element_type=F32) * decay
    idx = lax.broadcasted_iota(jnp.int32, (lc, 1), 0).astype(F32)
    state = state_ref[...]
    o = jnp.dot(scores.astype(BF16), v, preferred_element_type=F32)
    o = o + jnp.dot(q, state.astype(BF16), preferred_element_type=F32) * jnp.exp(lg * (idx + 1.0))
    kw = k.astype(F32) * jnp.exp(lg * (lc - 1.0 - idx))
    s_new = state * jnp.exp(lg * lc) + jnp.dot(kw.T.astype(BF16), v, preferred_element_type=F32)
    state_ref[...] = s_new

    g = g_ref[...]
    o_ref[...] = (_rms(o) * gain_ref[...] * (g * _sigmoid(g))).astype(o_ref.dtype)

    @pl.when(c == pl.num_programs(2) - 1)
    def _():
        s_out_ref[0, 0] = s_new


def _retention(q, k, v, g, s0, out_gain, seq, lc):
    b = s0.shape[0]
    nc = seq // lc
    log_gamma = jnp.log(1.0 - 2.0 ** (-5.0 - jnp.arange(RET_HEADS, dtype=F32)))
    lg = jnp.broadcast_to(log_gamma.reshape(RET_HEADS, 1, 1), (RET_HEADS, 1, LANES))
    tok = pl.BlockSpec((lc, RET_DV), lambda bi, h, c: (bi * nc + c, h))
    st = pl.BlockSpec((1, 1, RET_DK, RET_DV), lambda bi, h, c: (bi, h, 0, 0))
    return pl.pallas_call(
        functools.partial(_retention_body, lc),
        out_shape=(jax.ShapeDtypeStruct((b * seq, RET_WIDTH), BF16),
                   jax.ShapeDtypeStruct(s0.shape, F32)),
        grid=(b, RET_HEADS, nc),
        in_specs=[pl.BlockSpec((1, 1, LANES), lambda bi, h, c: (h, 0, 0)),
                  pl.BlockSpec((1, RET_DV), lambda bi, h, c: (0, 0)),
                  tok, tok, tok, tok, st],
        out_specs=(tok, st),
        scratch_shapes=[pltpu.VMEM((RET_DK, RET_DV), F32)],
        compiler_params=_params("parallel", "parallel", "arbitrary"),
        name="retention",
    )(lg, out_gain.reshape(1, RET_DV), q, k, v, g, s0)


def _route(logits):
    lane = lax.broadcasted_iota(jnp.int32, logits.shape, 1)
    big = jnp.int32(LANES)
    neg = -jnp.inf
    gl = jnp.where((lane >= N_EXPERTS) & (lane < N_EXPERTS + N_GROUPS), logits, neg)
    g_max = jnp.max(gl, axis=-1, keepdims=True)
    g_idx = jnp.min(jnp.where(gl == g_max, lane - N_EXPERTS, big), axis=-1, keepdims=True)
    g_w = 1.0 / jnp.sum(jnp.exp(gl - g_max), axis=-1, keepdims=True)
    in_group = (lane < N_EXPERTS) & (lax.shift_right_logical(lane, 3) == g_idx)
    el = jnp.where(in_group, logits, neg)
    v1 = jnp.max(el, axis=-1, keepdims=True)
    i1 = jnp.min(jnp.where(el == v1, lane, big), axis=-1, keepdims=True)
    el2 = jnp.where(lane == i1, neg, el)
    v2 = jnp.max(el2, axis=-1, keepdims=True)
    i2 = jnp.min(jnp.where(el2 == v2, lane, big), axis=-1, keepdims=True)
    e2 = jnp.exp(v2 - v1)
    w1 = g_w / (1.0 + e2)
    w2 = g_w * e2 / (1.0 + e2)
    gate = jnp.where(lane == i1, w1, 0.0) + jnp.where(lane == i2, w2, 0.0)
    sel = (jnp.where(lane == 0, w1, 0.0) + jnp.where(lane == 1, w2, 0.0)
           + jnp.where(lane == 2, i1.astype(F32), 0.0) + jnp.where(lane == 3, i2.astype(F32), 0.0))
    return gate, sel


def _outproj_body(x_ref, oda_ref, ort_ref, wo_ref, gf_ref, wr_hi_ref, wr_lo_ref, br_ref,
                  x1_ref, hf_ref, gate_ref, sel_ref):
    x1 = (x_ref[...]
          + jnp.dot(oda_ref[...], wo_ref[:DA_WIDTH, :], preferred_element_type=F32)
          + jnp.dot(ort_ref[...], wo_ref[DA_WIDTH:, :], preferred_element_type=F32))
    x1_ref[...] = x1
    hf = _rms(x1) * gf_ref[...]
    hf_hi = hf.astype(BF16)
    hf_ref[...] = hf
    hf_lo = (hf - hf_hi.astype(F32)).astype(BF16)
    wr_hi = wr_hi_ref[...]
    logits = (jnp.dot(hf_hi, wr_hi, preferred_element_type=F32)
              + jnp.dot(hf_lo, wr_hi, preferred_element_type=F32)
              + jnp.dot(hf_hi, wr_lo_ref[...], preferred_element_type=F32)
              + br_ref[...])
    gate_ref[...], sel_ref[...] = _route(logits)


def _outproj(x, o_da, o_ret, w_out, ffn_g, wr_hi, wr_lo, br, tm):
    t, d = x.shape
    row = lambda i: (i, 0)
    const = lambda i: (0, 0)
    return pl.pallas_call(
        _outproj_body,
        out_shape=(jax.ShapeDtypeStruct((t, d), F32), jax.ShapeDtypeStruct((t, d), F32),
                   jax.ShapeDtypeStruct((t, LANES), F32), jax.ShapeDtypeStruct((t, LANES), F32)),
        grid=(t // tm,),
        in_specs=[pl.BlockSpec((tm, d), row),
                  pl.BlockSpec((tm, DA_WIDTH), row),
                  pl.BlockSpec((tm, RET_WIDTH), row),
                  pl.BlockSpec((d, d), const),
                  pl.BlockSpec((1, d), const),
                  pl.BlockSpec((d, LANES), const),
                  pl.BlockSpec((d, LANES), const),
                  pl.BlockSpec((1, LANES), const)],
        out_specs=(pl.BlockSpec((tm, d), row), pl.BlockSpec((tm, d), row),
                   pl.BlockSpec((tm, LANES), row), pl.BlockSpec((tm, LANES), row)),
        compiler_params=_params("parallel"),
        name="out_proj_router",
    )(x, o_da, o_ret, w_out, ffn_g.reshape(1, d), wr_hi, wr_lo, br)


MOE_TILE = 256


def _row_copy(src_hbm, row, dst, dst_row, sem):
    return pltpu.make_async_copy(src_hbm.at[pl.ds(row, 1), :], dst.at[pl.ds(dst_row, 1), :], sem)


def _moe_expert_body(texp_ref, nused_ref, src_first, src_next, hf_hbm, wg_ref, wu_ref, wd_ref,
                     ys_ref, xbuf, sem):
    j = pl.program_id(0)
    slot = lax.rem(j, 2)
    nused = nused_ref[0]

    def gather(idx_ref, to_slot):
        def one(r, carry):
            c = _row_copy(hf_hbm, idx_ref[0, 0, r] if idx_ref is not None else 0,
                          xbuf.at[to_slot], r, sem.at[to_slot])
            c.start() if idx_ref is not None else c.wait()
            return carry
        lax.fori_loop(0, MOE_TILE, one, 0, unroll=8)

    @pl.when(j == 0)
    def _():
        gather(src_first, slot)

    @pl.when(j + 1 < nused)
    def _():
        gather(src_next, 1 - slot)

    @pl.when(j < nused)
    def _():
        gather(None, slot)
        x = xbuf[slot].astype(BF16)
        a = jnp.dot(x, wg_ref[0].astype(BF16), preferred_element_type=F32)
        u = jnp.dot(x, wu_ref[0].astype(BF16), preferred_element_type=F32)
        act = (a * _sigmoid(a)) * u
        ys_ref[...] = jnp.dot(act.astype(BF16), wd_ref[0].astype(BF16), preferred_element_type=F32)

    @pl.when(j >= nused)
    def _():
        ys_ref[...] = jnp.zeros(ys_ref.shape, F32)


def _moe_combine_body(pos_first, pos_next, x1_ref, w_ref, ys_hbm, y_ref, gbuf, sem):
    i = pl.program_id(0)
    slot = lax.rem(i, 2)
    tc = x1_ref.shape[0]

    def gather(idx_ref, to_slot):
        def one(r, carry):
            for k in range(2):
                c = _row_copy(ys_hbm, idx_ref[0, 0, 2 * r + k] if idx_ref is not None else 0,
                              gbuf.at[to_slot, k], r, sem.at[to_slot])
                c.start() if idx_ref is not None else c.wait()
            return carry
        lax.fori_loop(0, tc, one, 0, unroll=4)

    @pl.when(i == 0)
    def _():
        gather(pos_first, slot)

    @pl.when(i + 1 < pl.num_programs(0))
    def _():
        gather(pos_next, 1 - slot)

    gather(None, slot)
    w = w_ref[...]
    y_ref[...] = x1_ref[...] + w[:, 0:1] * gbuf[slot, 0] + w[:, 1:2] * gbuf[slot, 1]


def _moe_dense_body(hf_ref, gate_ref, x1_ref, wg_ref, wu_ref, wd_ref, y_ref):
    e = pl.program_id(1)

    @pl.when(e == 0)
    def _():
        y_ref[...] = x1_ref[...]

    h = hf_ref[...].astype(BF16)
    a = jnp.dot(h, wg_ref[0].astype(BF16), preferred_element_type=F32)
    u = jnp.dot(h, wu_ref[0].astype(BF16), preferred_element_type=F32)
    gate = gate_ref[...]
    lane = lax.broadcasted_iota(jnp.int32, gate.shape, 1)
    ge = jnp.sum(jnp.where(lane == e, gate, 0.0), axis=-1, keepdims=True)
    act = (a * _sigmoid(a)) * u * ge
    y_ref[...] += jnp.dot(act.astype(BF16), wd_ref[0].astype(BF16), preferred_element_type=F32)


def _moe_dense(hf, gate, x1, w_gate, w_up, w_down, tm):
    t, d = hf.shape
    row = lambda i, e: (i, 0)
    return pl.pallas_call(
        _moe_dense_body,
        out_shape=jax.ShapeDtypeStruct((t, d), F32),
        grid=(t // tm, N_EXPERTS),
        in_specs=[pl.BlockSpec((tm, d), row),
                  pl.BlockSpec((tm, LANES), row),
                  pl.BlockSpec((tm, d), row),
                  pl.BlockSpec((1, d, D_FF), lambda i, e: (e, 0, 0)),
                  pl.BlockSpec((1, d, D_FF), lambda i, e: (e, 0, 0)),
                  pl.BlockSpec((1, D_FF, d), lambda i, e: (e, 0, 0))],
        out_specs=pl.BlockSpec((tm, d), row),
        compiler_params=_params("parallel", "arbitrary"),
        name="moe_dense",
    )(hf, gate, x1, w_gate, w_up, w_down)


def _slot_sources_body(pos_ref, src_ref):
    def zero(s, carry):
        src_ref[s] = 0
        return carry
    lax.fori_loop(0, src_ref.shape[0], zero, 0, unroll=8)

    def put(p, carry):
        src_ref[pos_ref[p]] = lax.shift_right_logical(p, 1)
        return carry
    lax.fori_loop(0, pos_ref.shape[0], put, 0, unroll=8)


def _slot_sources(pos, n_pad):
    smem = pl.BlockSpec(memory_space=pltpu.SMEM)
    return pl.pallas_call(
        _slot_sources_body,
        out_shape=jax.ShapeDtypeStruct((n_pad,), jnp.int32),
        in_specs=[smem], out_specs=smem,
        name="moe_slot_sources",
    )(pos)


def _moe(hf, gate, sel, x1, w_gate, w_up, w_down, tm):
    t, d = hf.shape
    n_pair = 2 * t
    if n_pair < N_EXPERTS * MOE_TILE:
        return _moe_dense(hf, gate, x1, w_gate, w_up, w_down, tm)
    w_pair = sel[:, 0:2]
    e_flat = sel[:, 2:4].reshape(-1).astype(jnp.int32)
    n_pad = n_pair + N_EXPERTS * MOE_TILE
    n_tiles = n_pad // MOE_TILE
    onehot = (e_flat[:, None] == jnp.arange(N_EXPERTS, dtype=jnp.int32)[None, :]).astype(jnp.int32)
    csum = jnp.cumsum(onehot, axis=0)
    counts = csum[-1]
    rank = jnp.take_along_axis(csum, e_flat[:, None], axis=1)[:, 0] - 1
    padded = ((counts + MOE_TILE - 1) // MOE_TILE) * MOE_TILE
    ends = jnp.cumsum(padded)
    starts = ends - padded
    pos = (starts[e_flat] + rank).astype(jnp.int32)
    tile_start = jnp.arange(n_tiles, dtype=jnp.int32) * MOE_TILE
    tile_expert = jnp.minimum(jnp.searchsorted(ends, tile_start, side="right"),
                              N_EXPERTS - 1).astype(jnp.int32)
    n_used = (ends[-1] // MOE_TILE).astype(jnp.int32).reshape(1)
    src_tok = _slot_sources(pos, n_pad)

    def idx_specs(width, n_blocks):
        first = lambda *a: (0, 0, 0)
        nxt = lambda *a: (jnp.minimum(a[0] + 1, n_blocks - 1), 0, 0)
        return [pl.BlockSpec((1, 1, width), first, memory_space=pltpu.SMEM),
                pl.BlockSpec((1, 1, width), nxt, memory_space=pltpu.SMEM)]

    src3 = src_tok.reshape(n_tiles, 1, MOE_TILE)
    wspec = lambda shape: pl.BlockSpec(shape, lambda j, te, nu: (te[j], 0, 0))
    ys = pl.pallas_call(
        _moe_expert_body,
        out_shape=jax.ShapeDtypeStruct((n_pad, d), F32),
        grid_spec=pltpu.PrefetchScalarGridSpec(
            num_scalar_prefetch=2,
            grid=(n_tiles,),
            in_specs=idx_specs(MOE_TILE, n_tiles) + [
                pl.BlockSpec(memory_space=pl.ANY),
                wspec((1, d, D_FF)), wspec((1, d, D_FF)), wspec((1, D_FF, d))],
            out_specs=pl.BlockSpec((MOE_TILE, d), lambda j, te, nu: (j, 0)),
            scratch_shapes=[pltpu.VMEM((2, MOE_TILE, d), F32), pltpu.SemaphoreType.DMA((2,))]),
        compiler_params=_params("arbitrary"),
        name="moe_experts",
    )(tile_expert, n_used, src3, src3, hf, w_gate, w_up, w_down)

    tc = min(tm, MOE_TILE)
    pos3 = pos.reshape(t // tc, 1, 2 * tc)
    return pl.pallas_call(
        _moe_combine_body,
        out_shape=jax.ShapeDtypeStruct((t, d), F32),
        grid=(t // tc,),
        in_specs=idx_specs(2 * tc, t // tc) + [
            pl.BlockSpec((tc, d), lambda i: (i, 0)),
            pl.BlockSpec((tc, 2), lambda i: (i, 0)),
            pl.BlockSpec(memory_space=pl.ANY)],
        out_specs=pl.BlockSpec((tc, d), lambda i: (i, 0)),
        scratch_shapes=[pltpu.VMEM((2, 2, tc, d), F32), pltpu.SemaphoreType.DMA((2,))],
        compiler_params=_params("arbitrary"),
        name="moe_combine",
    )(pos3, pos3, x1, w_pair, ys)


def _split_hi_lo(w):
    hi = w.astype(BF16)
    return hi, (w - hi.astype(F32)).astype(BF16)


def kernel(x_prompt, x_sample, cache_k_diff, cache_v_diff, state_retention, attn_norm_g, w_in, da_q_norm_g, da_k_norm_g, da_lambda_q1, da_lambda_k1, da_lambda_q2, da_lambda_k2, da_out_norm_g, ret_out_norm_g, w_out, ffn_norm_g, w_group, b_group, w_expert, b_expert, w_gate, w_up, w_down):
    assert w_in.shape[0] == 1, "single-layer model"
    bp, seq, d = x_prompt.shape
    bd, t_dec, _ = x_sample.shape
    past = cache_k_diff.shape[2]
    assert bp == 1

    w_in_b = w_in[0].astype(BF16)
    w_out_b = w_out[0].astype(BF16)
    w_router = jnp.concatenate([w_expert[0].reshape(d, N_EXPERTS), w_group[0]], axis=1)
    w_router = jnp.pad(w_router, ((0, 0), (0, LANES - w_router.shape[1])))
    wr_hi, wr_lo = _split_hi_lo(w_router)
    b_router = jnp.concatenate([b_expert[0].reshape(N_EXPERTS), b_group[0]])
    b_router = jnp.pad(b_router, (0, LANES - b_router.shape[0])).reshape(1, LANES)

    lams = _lambda_args(da_lambda_q1[0], da_lambda_k1[0], da_lambda_q2[0], da_lambda_k2[0])

    def layer(x, cos, sin, attend, want_vt, ret_seq, ret_chunk, s0, tm):
        h = _rmsnorm(x, attn_norm_g[0], tm)
        q_da, k_da, k_da_b, v_da, v_da_b, v_da_t, q_r, k_r, v_r, g_r = _project(
            h, w_in_b, da_q_norm_g[0], da_k_norm_g[0], cos, sin, tm, want_vt)
        o_da = attend(q_da, k_da_b, v_da_b, v_da_t)
        o_ret, s_new = _retention(q_r, k_r, v_r, g_r, s0, ret_out_norm_g[0], ret_seq, ret_chunk)
        x1, hf, gate, sel = _outproj(x, o_da, o_ret, w_out_b, ffn_norm_g[0], wr_hi, wr_lo,
                                     b_router, min(tm, 256))
        y = _moe(hf, gate, sel, x1, w_gate[0], w_up[0], w_down[0], tm)
        return y, k_da, v_da, s_new

    cos_p, sin_p = _rope_tables(seq, 0)
    y_p, k_p, v_p, s_p = layer(
        x_prompt.reshape(seq, d), cos_p, sin_p,
        lambda q, k, v, vt: _da_prompt(q, k, vt[0], lams, da_out_norm_g[0], 512),
        True, seq, 256, jnp.zeros((bp, RET_HEADS, RET_DK, RET_DV), F32), 512)

    cos_s, sin_s = _rope_tables(t_dec, past)
    cos_s = jnp.tile(cos_s, (bd, 1))
    sin_s = jnp.tile(sin_s, (bd, 1))
    y_s, k_s, v_s, s_s = layer(
        x_sample.reshape(bd * t_dec, d), cos_s, sin_s,
        lambda q, k, v, vt: _da_sample(q, k, v, cache_k_diff[0], cache_v_diff[0], lams,
                                       da_out_norm_g[0], t_dec, 1024),
        False, t_dec, t_dec, state_retention[0], bd * t_dec)

    return (y_p.reshape(bp, seq, d),
            y_s.reshape(bd, t_dec, d),
            k_p.reshape(1, bp, seq, DA_HEADS, 2 * DA_DH),
            v_p.reshape(1, bp, seq, DA_HEADS, DA_DV),
            s_p.reshape(1, bp, RET_HEADS, RET_DK, RET_DV),
            k_s.reshape(1, bd, t_dec, DA_HEADS, 2 * DA_DH),
            v_s.reshape(1, bd, t_dec, DA_HEADS, DA_DV),
            s_s.reshape(1, bd, RET_HEADS, RET_DK, RET_DV))
```

```python
import functools
import math

import jax
import jax.numpy as jnp
from jax import lax
from jax.experimental import pallas as pl
from jax.experimental.pallas import tpu as pltpu

D_MODEL = 2048
CHUNK = 64
DA_HEADS = 4
DA_DH = 128
DA_DV = 2 * DA_DH
DA_WIDTH = DA_HEADS * DA_DV
RET_HEADS = 4
RET_DK = 256
RET_DV = 256
RET_WIDTH = RET_HEADS * RET_DV
IN_GROUP = 1024
N_GROUPS = 4
EXP_PER_GROUP = 8
N_EXPERTS = N_GROUPS * EXP_PER_GROUP
D_FF = D_MODEL // 8
EPS = 1e-6
NEG_INF = -1e30
ROPE_BASE = 10000.0
LAM_INIT = 0.8 - 0.6 * math.exp(-0.3 * 0)

LANES = 128
VMEM_LIMIT = 48 * 1024 * 1024

F32 = jnp.float32
BF16 = jnp.bfloat16


def _params(*sem):
    return pltpu.CompilerParams(dimension_semantics=sem, vmem_limit_bytes=VMEM_LIMIT)


def _sigmoid(x):
    return 1.0 / (1.0 + jnp.exp(-x))


def _rms(x):
    return x * lax.rsqrt(jnp.mean(x * x, axis=-1, keepdims=True) + EPS)


def _rmsnorm_body(x_ref, g_ref, o_ref):
    o_ref[...] = (_rms(x_ref[...]) * g_ref[...]).astype(o_ref.dtype)


def _rmsnorm(x, g, tm):
    t, d = x.shape
    return pl.pallas_call(
        _rmsnorm_body,
        out_shape=jax.ShapeDtypeStruct((t, d), BF16),
        grid=(t // tm,),
        in_specs=[pl.BlockSpec((tm, d), lambda i: (i, 0)),
                  pl.BlockSpec((1, d), lambda i: (0, 0))],
        out_specs=pl.BlockSpec((tm, d), lambda i: (i, 0)),
        compiler_params=_params("parallel"),
        name="attn_norm",
    )(x, g.reshape(1, d))


def _rope_table_body(pos0, tr, invf_ref, cos_ref, sin_ref):
    row = lax.broadcasted_iota(jnp.int32, (tr, LANES), 0) + (pl.program_id(0) * tr + pos0)
    ang = row.astype(F32) * invf_ref[...]
    cos_ref[...] = jnp.cos(ang)
    sin_ref[...] = jnp.sin(ang)


def _rope_tables(n_pos, pos0):
    half = RET_DK // 2
    inv_freq = (ROPE_BASE ** (-jnp.arange(half, dtype=F32) / half)).reshape(1, half)
    tr = min(n_pos, 512)
    spec = pl.BlockSpec((tr, half), lambda i: (i, 0))
    return pl.pallas_call(
        functools.partial(_rope_table_body, pos0, tr),
        out_shape=(jax.ShapeDtypeStruct((n_pos, half), F32),) * 2,
        grid=(n_pos // tr,),
        in_specs=[pl.BlockSpec((1, half), lambda i: (0, 0))],
        out_specs=(spec, spec),
        compiler_params=_params("parallel"),
        name="rope_tables",
    )(inv_freq)


def _store_cols(o, sl, val):
    if len(o.shape) == 2:
        o[:, sl] = val.astype(o.dtype)
    else:
        hd, off = divmod(sl.start, DA_DV)
        o[:, hd, off:off + (sl.stop - sl.start)] = val.astype(o.dtype)


def _store_all(z, outs):
    for o in outs:
        for hd in range(DA_HEADS):
            sl = slice(hd * DA_DV, (hd + 1) * DA_DV)
            _store_cols(o, sl, z[:, sl])


def _proj_plain_body(h_ref, w_ref, *outs):
    z = jnp.dot(h_ref[...], w_ref[...], preferred_element_type=F32)
    _store_all(z, outs)


def _proj_plain_t_body(h_ref, w_ref, *outs):
    z = jnp.dot(h_ref[...], w_ref[...], preferred_element_type=F32)
    _store_all(z, outs[:-1])
    outs[-1][...] = z.T.astype(outs[-1].dtype)


def _proj_qknorm_body(scale, h_ref, w_ref, g_ref, *outs):
    z = jnp.dot(h_ref[...], w_ref[...], preferred_element_type=F32)
    g = g_ref[...]
    for c in range(IN_GROUP // DA_DH):
        sl = slice(c * DA_DH, (c + 1) * DA_DH)
        zc = _rms(z[:, sl]) * g
        for o in outs:
            _store_cols(o, sl, zc * scale if o.dtype == BF16 else zc)


def _proj_rotary_body(scale, h_ref, w_ref, cos_ref, sin_ref, o_ref):
    z = jnp.dot(h_ref[...], w_ref[...], preferred_element_type=F32) * scale
    cos = cos_ref[...]
    sin = sin_ref[...]
    half = RET_DK // 2
    for hd in range(RET_HEADS):
        x1 = z[:, hd * RET_DK: hd * RET_DK + half]
        x2 = z[:, hd * RET_DK + half: (hd + 1) * RET_DK]
        o_ref[:, hd * RET_DK: hd * RET_DK + half] = (x1 * cos - x2 * sin).astype(o_ref.dtype)
        o_ref[:, hd * RET_DK + half: (hd + 1) * RET_DK] = (x1 * sin + x2 * cos).astype(o_ref.dtype)


def _proj(h, w_in, group, body, extra, extra_specs, out_dtypes, tm, name, per_head=False,
          transposed_out=False):
    t, d = h.shape
    row = lambda i: (i, 0)
    split = lambda dt: per_head and dt == F32
    shape = lambda dt: (t, DA_HEADS, DA_DV) if split(dt) else (t, IN_GROUP)
    block = lambda dt: (pl.BlockSpec((tm, DA_HEADS, DA_DV), lambda i: (i, 0, 0)) if split(dt)
                        else pl.BlockSpec((tm, IN_GROUP), row))
    outs = tuple(jax.ShapeDtypeStruct(shape(dt), dt) for dt in out_dtypes)
    out_specs = tuple(block(dt) for dt in out_dtypes)
    if transposed_out:
        outs += (jax.ShapeDtypeStruct((IN_GROUP, t), BF16),)
        out_specs += (pl.BlockSpec((IN_GROUP, tm), lambda i: (0, i)),)
    res = pl.pallas_call(
        body,
        out_shape=outs,
        grid=(t // tm,),
        in_specs=[pl.BlockSpec((tm, d), row),
                  pl.BlockSpec((d, IN_GROUP), lambda i: (0, group))] + extra_specs,
        out_specs=out_specs,
        compiler_params=_params("parallel"),
        name=name,
    )(h, w_in, *extra)
    return res


def _project(h, w_in, q_norm_g, k_norm_g, cos, sin, tm, want_vt):
    half = RET_DK // 2
    row = lambda i: (i, 0)
    gspec = [pl.BlockSpec((1, DA_DH), lambda i: (0, 0))]
    rspec = [pl.BlockSpec((tm, half), row), pl.BlockSpec((tm, half), row)]
    (q_da,) = _proj(h, w_in, 0, functools.partial(_proj_qknorm_body, DA_DH ** -0.5),
                    [q_norm_g.reshape(1, DA_DH)], gspec, [BF16], tm, "proj_q_da")
    k_da, k_da_b = _proj(h, w_in, 1, functools.partial(_proj_qknorm_body, 1.0),
                         [k_norm_g.reshape(1, DA_DH)], gspec, [F32, BF16], tm, "proj_k_da",
                         per_head=True)
    v_da, v_da_b, *v_da_t = _proj(h, w_in, 2, _proj_plain_t_body if want_vt else _proj_plain_body,
                                  [], [], [F32, BF16], tm, "proj_v_da", per_head=True,
                                  transposed_out=want_vt)
    (q_r,) = _proj(h, w_in, 3, functools.partial(_proj_rotary_body, 1.0),
                   [cos, sin], rspec, [BF16], tm, "proj_q_ret")
    (k_r,) = _proj(h, w_in, 4, functools.partial(_proj_rotary_body, RET_DK ** -0.5),
                   [cos, sin], rspec, [BF16], tm, "proj_k_ret")
    (v_r,) = _proj(h, w_in, 5, _proj_plain_body, [], [], [BF16], tm, "proj_v_ret")
    (g_r,) = _proj(h, w_in, 6, _proj_plain_body, [], [], [F32], tm, "proj_g_ret")
    return q_da, k_da, k_da_b, v_da, v_da_b, v_da_t, q_r, k_r, v_r, g_r


def _diff_lambda(lq1, lk1, lq2, lk2):
    s1 = jnp.sum(lq1[...] * lk1[...], axis=-1, keepdims=True)
    s2 = jnp.sum(lq2[...] * lk2[...], axis=-1, keepdims=True)
    return jnp.exp(s1) - jnp.exp(s2) + LAM_INIT


def _softmax_step(q, k, v, m_ref, l_ref, acc_ref, idx, mask):
    s = lax.dot_general(q, k, (((1,), (1,)), ((), ())), preferred_element_type=F32)
    if mask is not None:
        s = jnp.where(mask, s, NEG_INF)
    m_prev = m_ref[idx]
    m_new = jnp.maximum(m_prev, jnp.max(s, axis=-1, keepdims=True))
    alpha = jnp.exp(m_prev - m_new)
    p = jnp.exp(s - m_new)
    l_ref[idx] = alpha * l_ref[idx] + jnp.sum(p, axis=-1, keepdims=True)
    acc_ref[idx] = alpha * acc_ref[idx] + jnp.dot(p.astype(BF16), v, preferred_element_type=F32)
    m_ref[idx] = m_new


def _softmax_init(m_ref, l_ref, acc_ref):
    m_ref[...] = jnp.full(m_ref.shape, NEG_INF, F32)
    l_ref[...] = jnp.zeros(l_ref.shape, F32)
    acc_ref[...] = jnp.zeros(acc_ref.shape, F32)


def _diff_combine(lam, gain, l_ref, acc_ref, i0, i1):
    o = acc_ref[i0] / l_ref[i0] - lam * (acc_ref[i1] / l_ref[i1])
    return _rms(o) * gain * (1.0 - LAM_INIT)


def _da_prompt_body(tq, lq1, lk1, lq2, lk2, gain_ref, q_ref, k_ref, vt_ref, o_ref,
                    m_ref, l_ref, acc_ref):
    qi = pl.program_id(1)
    _softmax_init(m_ref, l_ref, acc_ref)

    def block(start, mask):
        k = k_ref[pl.ds(start, tq), :]
        vt = vt_ref[:, pl.ds(start, tq)]
        sts = []
        for mp in range(2):
            sl = slice(mp * DA_DH, (mp + 1) * DA_DH)
            sts.append(lax.dot_general(k[:, sl], q_ref[:, sl], (((1,), (1,)), ((), ())),
                                       preferred_element_type=F32))
        pts, alphas = [], []
        for mp in range(2):
            st = sts[mp]
            if mask is not None:
                st = jnp.where(mask, st, NEG_INF)
            m_prev = m_ref[mp]
            m_new = jnp.maximum(m_prev, jnp.max(st, axis=0, keepdims=True))
            alpha = jnp.exp(m_prev - m_new)
            pt = jnp.exp(st - m_new)
            l_ref[mp] = alpha * l_ref[mp] + jnp.sum(pt, axis=0, keepdims=True)
            m_ref[mp] = m_new
            pts.append(pt.astype(BF16))
            alphas.append(alpha)
        for mp in range(2):
            acc_ref[mp] = alphas[mp] * acc_ref[mp] + jnp.dot(vt, pts[mp],
                                                             preferred_element_type=F32)

    def full_block(kb, carry):
        block(pl.multiple_of(kb * tq, tq), None)
        return carry

    lax.fori_loop(0, qi, full_block, 0)
    kc = lax.shift_right_logical(lax.broadcasted_iota(jnp.int32, (tq, tq), 0), 6)
    qc = lax.shift_right_logical(lax.broadcasted_iota(jnp.int32, (tq, tq), 1), 6)
    block(pl.multiple_of(qi * tq, tq), kc <= qc)

    lam = _diff_lambda(lq1, lk1, lq2, lk2)
    ot = acc_ref[0] / l_ref[0] - lam * (acc_ref[1] / l_ref[1])
    ot = ot * lax.rsqrt(jnp.mean(ot * ot, axis=0, keepdims=True) + EPS)
    ot = ot * gain_ref[...] * (1.0 - LAM_INIT)
    o_ref[...] = ot.T.astype(o_ref.dtype)


def _lambda_args(lq1, lk1, lq2, lk2):
    return [a.reshape(1, DA_DH) for a in (lq1, lk1, lq2, lk2)]


def _da_prompt(q, k, vt, lams, out_gain, tq):
    s = q.shape[0]
    assert CHUNK == 64 and tq % CHUNK == 0 and s % tq == 0
    const = lambda h, i: (0, 0)
    vec = pl.BlockSpec((1, DA_DH), const)
    return pl.pallas_call(
        functools.partial(_da_prompt_body, tq),
        out_shape=jax.ShapeDtypeStruct((s, DA_WIDTH), BF16),
        grid=(DA_HEADS, s // tq),
        in_specs=[vec, vec, vec, vec,
                  pl.BlockSpec((DA_DV, 1), const),
                  pl.BlockSpec((tq, DA_DV), lambda h, i: (i, h)),
                  pl.BlockSpec((s, DA_DV), lambda h, i: (0, h)),
                  pl.BlockSpec((DA_DV, s), lambda h, i: (h, 0))],
        out_specs=pl.BlockSpec((tq, DA_DV), lambda h, i: (i, h)),
        scratch_shapes=[pltpu.VMEM((2, 1, tq), F32), pltpu.VMEM((2, 1, tq), F32),
                        pltpu.VMEM((2, DA_DV, tq), F32)],
        compiler_params=_params("parallel", "parallel"),
        name="diff_attn_prompt",
    )(*lams, out_gain.reshape(DA_DV, 1), q, k, vt)


def _da_sample_body(tk, lq1, lk1, lq2, lk2, gain_ref, q_ref, kn_ref, vn_ref, ck_hbm, cv_hbm, o_ref,
                    kbuf, vbuf, sem, m_ref, l_ref, acc_ref):
    n = pl.program_id(0) * DA_HEADS + pl.program_id(1)
    total = pl.num_programs(0) * DA_HEADS
    slot = lax.rem(n, 2)

    def copies(step, to_slot):
        stream = lax.div(step, DA_HEADS)
        head = lax.rem(step, DA_HEADS)
        return (pltpu.make_async_copy(ck_hbm.at[stream, :, head, :], kbuf.at[to_slot], sem.at[0, to_slot]),
                pltpu.make_async_copy(cv_hbm.at[stream, :, head, :], vbuf.at[to_slot], sem.at[1, to_slot]))

    @pl.when(n == 0)
    def _():
        for c in copies(n, slot):
            c.start()

    @pl.when(n + 1 < total)
    def _():
        for c in copies(n + 1, 1 - slot):
            c.start()

    def attend(k, v):
        for mp in range(2):
            sl = slice(mp * DA_DH, (mp + 1) * DA_DH)
            _softmax_step(q_ref[:, sl], k[:, sl], v, m_ref, l_ref, acc_ref, mp, None)

    _softmax_init(m_ref, l_ref, acc_ref)
    attend(kn_ref[...], vn_ref[...])
    for c in copies(n, slot):
        c.wait()
    for j in range(kbuf.shape[1] // tk):
        attend(kbuf[slot, j * tk:(j + 1) * tk, :].astype(BF16),
               vbuf[slot, j * tk:(j + 1) * tk, :].astype(BF16))
    lam = _diff_lambda(lq1, lk1, lq2, lk2)
    o_ref[...] = _diff_combine(lam, gain_ref[...], l_ref, acc_ref, 0, 1).astype(o_ref.dtype)


def _da_sample(q, k_new, v_new, cache_k, cache_v, lams, out_gain, t, tk):
    b, p = cache_k.shape[:2]
    const = lambda i, h: (0, 0)
    vec = pl.BlockSpec((1, DA_DH), const)
    tok = pl.BlockSpec((t, DA_DV), lambda i, h: (i, h))
    hbm = pl.BlockSpec(memory_space=pl.ANY)
    return pl.pallas_call(
        functools.partial(_da_sample_body, tk),
        out_shape=jax.ShapeDtypeStruct((b * t, DA_WIDTH), BF16),
        grid=(b, DA_HEADS),
        in_specs=[vec, vec, vec, vec, pl.BlockSpec((1, DA_DV), const), tok, tok, tok, hbm, hbm],
        out_specs=tok,
        scratch_shapes=[pltpu.VMEM((2, p, DA_DV), F32), pltpu.VMEM((2, p, DA_DV), F32),
                        pltpu.SemaphoreType.DMA((2, 2)),
                        pltpu.VMEM((2, t, 1), F32), pltpu.VMEM((2, t, 1), F32),
                        pltpu.VMEM((2, t, DA_DV), F32)],
        compiler_params=_params("arbitrary", "arbitrary"),
        name="diff_attn_sample",
    )(*lams, out_gain.reshape(1, DA_DV), q, k_new, v_new, cache_k, cache_v)


def _retention_body(lc, lg_ref, gain_ref, q_ref, k_ref, v_ref, g_ref, s0_ref, o_ref, s_out_ref,
                    state_ref):
    c = pl.program_id(2)

    @pl.when(c == 0)
    def _():
        state_ref[...] = s0_ref[0, 0]

    lg = lg_ref[0][:, :1]
    q = q_ref[...]
    k = k_ref[...]
    v = v_ref[...]
    rel = (lax.broadcasted_iota(jnp.int32, (lc, lc), 0)
           - lax.broadcasted_iota(jnp.int32, (lc, lc), 1)).astype(F32)
    decay = jnp.where(rel >= 0, jnp.exp(lg * jnp.maximum(rel, 0.0)), 0.0)
    scores = lax.dot_general(q, k, (((1,), (1,)), ((), ())), preferred_element_type=F32) * decay
    idx = lax.broadcasted_iota(jnp.int32, (lc, 1), 0).astype(F32)
    state = state_ref[...]
    o = jnp.dot(scores.astype(BF16), v, preferred_element_type=F32)
    o = o + jnp.dot(q, state.astype(BF16), preferred_element_type=F32) * jnp.exp(lg * (idx + 1.0))
    kw = k.astype(F32) * jnp.exp(lg * (lc - 1.0 - idx))
    s_new = state * jnp.exp(lg * lc) + jnp.dot(kw.T.astype(BF16), v, preferred_element_type=F32)
    state_ref[...] = s_new

    g = g_ref[...]
    o_ref[...] = (_rms(o) * gain_ref[...] * (g * _sigmoid(g))).astype(o_ref.dtype)

    @pl.when(c == pl.num_programs(2) - 1)
    def _():
        s_out_ref[0, 0] = s_new


def _retention(q, k, v, g, s0, out_gain, seq, lc):
    b = s0.shape[0]
    nc = seq // lc
    log_gamma = jnp.log(1.0 - 2.0 ** (-5.0 - jnp.arange(RET_HEADS, dtype=F32)))
    lg = jnp.broadcast_to(log_gamma.reshape(RET_HEADS, 1, 1), (RET_HEADS, 1, LANES))
    tok = pl.BlockSpec((lc, RET_DV), lambda bi, h, c: (bi * nc + c, h))
    st = pl.BlockSpec((1, 1, RET_DK, RET_DV), lambda bi, h, c: (bi, h, 0, 0))
    return pl.pallas_call(
        functools.partial(_retention_body, lc),
        out_shape=(jax.ShapeDtypeStruct((b * seq, RET_WIDTH), BF16),
                   jax.ShapeDtypeStruct(s0.shape, F32)),
        grid=(b, RET_HEADS, nc),
        in_specs=[pl.BlockSpec((1, 1, LANES), lambda bi, h, c: (h, 0, 0)),
                  pl.BlockSpec((1, RET_DV), lambda bi, h, c: (0, 0)),
                  tok, tok, tok, tok, st],
        out_specs=(tok, st),
        scratch_shapes=[pltpu.VMEM((RET_DK, RET_DV), F32)],
        compiler_params=_params("parallel", "parallel", "arbitrary"),
        name="retention",
    )(lg, out_gain.reshape(1, RET_DV), q, k, v, g, s0)


def _route(logits):
    lane = lax.broadcasted_iota(jnp.int32, logits.shape, 1)
    big = jnp.int32(LANES)
    neg = -jnp.inf
    gl = jnp.where((lane >= N_EXPERTS) & (lane < N_EXPERTS + N_GROUPS), logits, neg)
    g_max = jnp.max(gl, axis=-1, keepdims=True)
    g_idx = jnp.min(jnp.where(gl == g_max, lane - N_EXPERTS, big), axis=-1, keepdims=True)
    g_w = 1.0 / jnp.sum(jnp.exp(gl - g_max), axis=-1, keepdims=True)
    in_group = (lane < N_EXPERTS) & (lax.shift_right_logical(lane, 3) == g_idx)
    el = jnp.where(in_group, logits, neg)
    v1 = jnp.max(el, axis=-1, keepdims=True)
    i1 = jnp.min(jnp.where(el == v1, lane, big), axis=-1, keepdims=True)
    el2 = jnp.where(lane == i1, neg, el)
    v2 = jnp.max(el2, axis=-1, keepdims=True)
    i2 = jnp.min(jnp.where(el2 == v2, lane, big), axis=-1, keepdims=True)
    e2 = jnp.exp(v2 - v1)
    w1 = g_w / (1.0 + e2)
    w2 = g_w * e2 / (1.0 + e2)
    gate = jnp.where(lane == i1, w1, 0.0) + jnp.where(lane == i2, w2, 0.0)
    sel = (jnp.where(lane == 0, w1, 0.0) + jnp.where(lane == 1, w2, 0.0)
           + jnp.where(lane == 2, i1.astype(F32), 0.0) + jnp.where(lane == 3, i2.astype(F32), 0.0))
    return gate, sel


def _outproj_body(x_ref, oda_ref, ort_ref, wo_ref, gf_ref, wr_hi_ref, wr_lo_ref, br_ref,
                  x1_ref, hf_ref, gate_ref, sel_ref):
    x1 = (x_ref[...]
          + jnp.dot(oda_ref[...], wo_ref[:DA_WIDTH, :], preferred_element_type=F32)
          + jnp.dot(ort_ref[...], wo_ref[DA_WIDTH:, :], preferred_element_type=F32))
    x1_ref[...] = x1
    hf = _rms(x1) * gf_ref[...]
    hf_hi = hf.astype(BF16)
    hf_ref[...] = hf
    hf_lo = (hf - hf_hi.astype(F32)).astype(BF16)
    wr_hi = wr_hi_ref[...]
    logits = (jnp.dot(hf_hi, wr_hi, preferred_element_type=F32)
              + jnp.dot(hf_lo, wr_hi, preferred_element_type=F32)
              + jnp.dot(hf_hi, wr_lo_ref[...], preferred_element_type=F32)
              + br_ref[...])
    gate_ref[...], sel_ref[...] = _route(logits)


def _outproj(x, o_da, o_ret, w_out, ffn_g, wr_hi, wr_lo, br, tm):
    t, d = x.shape
    row = lambda i: (i, 0)
    const = lambda i: (0, 0)
    return pl.pallas_call(
        _outproj_body,
        out_shape=(jax.ShapeDtypeStruct((t, d), F32), jax.ShapeDtypeStruct((t, d), F32),
                   jax.ShapeDtypeStruct((t, LANES), F32), jax.ShapeDtypeStruct((t, LANES), F32)),
        grid=(t // tm,),
        in_specs=[pl.BlockSpec((tm, d), row),
                  pl.BlockSpec((tm, DA_WIDTH), row),
                  pl.BlockSpec((tm, RET_WIDTH), row),
                  pl.BlockSpec((d, d), const),
                  pl.BlockSpec((1, d), const),
                  pl.BlockSpec((d, LANES), const),
                  pl.BlockSpec((d, LANES), const),
                  pl.BlockSpec((1, LANES), const)],
        out_specs=(pl.BlockSpec((tm, d), row), pl.BlockSpec((tm, d), row),
                   pl.BlockSpec((tm, LANES), row), pl.BlockSpec((tm, LANES), row)),
        compiler_params=_params("parallel"),
        name="out_proj_router",
    )(x, o_da, o_ret, w_out, ffn_g.reshape(1, d), wr_hi, wr_lo, br)


MOE_TILE = 256


def _row_copy(src_hbm, row, dst, dst_row, sem):
    return pltpu.make_async_copy(src_hbm.at[pl.ds(row, 1), :], dst.at[pl.ds(dst_row, 1), :], sem)


def _moe_expert_body(texp_ref, nused_ref, src_first, src_next, hf_hbm, wg_ref, wu_ref, wd_ref,
                     ys_ref, xbuf, sem):
    j = pl.program_id(0)
    slot = lax.rem(j, 2)
    nused = nused_ref[0]

    def gather(idx_ref, to_slot):
        def one(r, carry):
            c = _row_copy(hf_hbm, idx_ref[0, 0, r] if idx_ref is not None else 0,
                          xbuf.at[to_slot], r, sem.at[to_slot])
            c.start() if idx_ref is not None else c.wait()
            return carry
        lax.fori_loop(0, MOE_TILE, one, 0, unroll=8)

    @pl.when(j == 0)
    def _():
        gather(src_first, slot)

    @pl.when(j + 1 < nused)
    def _():
        gather(src_next, 1 - slot)

    @pl.when(j < nused)
    def _():
        gather(None, slot)
        x = xbuf[slot].astype(BF16)
        a = jnp.dot(x, wg_ref[0].astype(BF16), preferred_element_type=F32)
        u = jnp.dot(x, wu_ref[0].astype(BF16), preferred_element_type=F32)
        act = (a * _sigmoid(a)) * u
        ys_ref[...] = jnp.dot(act.astype(BF16), wd_ref[0].astype(BF16), preferred_element_type=F32)

    @pl.when(j >= nused)
    def _():
        ys_ref[...] = jnp.zeros(ys_ref.shape, F32)


def _moe_combine_body(pos_first, pos_next, x1_ref, w_ref, ys_hbm, y_ref, gbuf, sem):
    i = pl.program_id(0)
    slot = lax.rem(i, 2)
    tc = x1_ref.shape[0]

    def gather(idx_ref, to_slot):
        def one(r, carry):
            for k in range(2):
                c = _row_copy(ys_hbm, idx_ref[0, 0, 2 * r + k] if idx_ref is not None else 0,
                              gbuf.at[to_slot, k], r, sem.at[to_slot])
                c.start() if idx_ref is not None else c.wait()
            return carry
        lax.fori_loop(0, tc, one, 0, unroll=4)

    @pl.when(i == 0)
    def _():
        gather(pos_first, slot)

    @pl.when(i + 1 < pl.num_programs(0))
    def _():
        gather(pos_next, 1 - slot)

    gather(None, slot)
    w = w_ref[...]
    y_ref[...] = x1_ref[...] + w[:, 0:1] * gbuf[slot, 0] + w[:, 1:2] * gbuf[slot, 1]


def _moe_dense_body(hf_ref, gate_ref, x1_ref, wg_ref, wu_ref, wd_ref, y_ref):
    e = pl.program_id(1)

    @pl.when(e == 0)
    def _():
        y_ref[...] = x1_ref[...]

    h = hf_ref[...].astype(BF16)
    a = jnp.dot(h, wg_ref[0].astype(BF16), preferred_element_type=F32)
    u = jnp.dot(h, wu_ref[0].astype(BF16), preferred_element_type=F32)
    gate = gate_ref[...]
    lane = lax.broadcasted_iota(jnp.int32, gate.shape, 1)
    ge = jnp.sum(jnp.where(lane == e, gate, 0.0), axis=-1, keepdims=True)
    act = (a * _sigmoid(a)) * u * ge
    y_ref[...] += jnp.dot(act.astype(BF16), wd_ref[0].astype(BF16), preferred_element_type=F32)


def _moe_dense(hf, gate, x1, w_gate, w_up, w_down, tm):
    t, d = hf.shape
    row = lambda i, e: (i, 0)
    return pl.pallas_call(
        _moe_dense_body,
        out_shape=jax.ShapeDtypeStruct((t, d), F32),
        grid=(t // tm, N_EXPERTS),
        in_specs=[pl.BlockSpec((tm, d), row),
                  pl.BlockSpec((tm, LANES), row),
                  pl.BlockSpec((tm, d), row),
                  pl.BlockSpec((1, d, D_FF), lambda i, e: (e, 0, 0)),
                  pl.BlockSpec((1, d, D_FF), lambda i, e: (e, 0, 0)),
                  pl.BlockSpec((1, D_FF, d), lambda i, e: (e, 0, 0))],
        out_specs=pl.BlockSpec((tm, d), row),
        compiler_params=_params("parallel", "arbitrary"),
        name="moe_dense",
    )(hf, gate, x1, w_gate, w_up, w_down)


def _slot_sources_body(pos_ref, src_ref):
    def zero(s, carry):
        src_ref[s] = 0
        return carry
    lax.fori_loop(0, src_ref.shape[0], zero, 0, unroll=8)

    def put(p, carry):
        src_ref[pos_ref[p]] = lax.shift_right_logical(p, 1)
        return carry
    lax.fori_loop(0, pos_ref.shape[0], put, 0, unroll=8)


def _slot_sources(pos, n_pad):
    smem = pl.BlockSpec(memory_space=pltpu.SMEM)
    return pl.pallas_call(
        _slot_sources_body,
        out_shape=jax.ShapeDtypeStruct((n_pad,), jnp.int32),
        in_specs=[smem], out_specs=smem,
        name="moe_slot_sources",
    )(pos)


def _moe(hf, gate, sel, x1, w_gate, w_up, w_down, tm):
    t, d = hf.shape
    n_pair = 2 * t
    if n_pair < N_EXPERTS * MOE_TILE:
        return _moe_dense(hf, gate, x1, w_gate, w_up, w_down, tm)
    w_pair = sel[:, 0:2]
    e_flat = sel[:, 2:4].reshape(-1).astype(jnp.int32)
    n_pad = n_pair + N_EXPERTS * MOE_TILE
    n_tiles = n_pad // MOE_TILE
    onehot = (e_flat[:, None] == jnp.arange(N_EXPERTS, dtype=jnp.int32)[None, :]).astype(jnp.int32)
    csum = jnp.cumsum(onehot, axis=0)
    counts = csum[-1]
    rank = jnp.take_along_axis(csum, e_flat[:, None], axis=1)[:, 0] - 1
    padded = ((counts + MOE_TILE - 1) // MOE_TILE) * MOE_TILE
    ends = jnp.cumsum(padded)
    starts = ends - padded
    pos = (starts[e_flat] + rank).astype(jnp.int32)
    tile_start = jnp.arange(n_tiles, dtype=jnp.int32) * MOE_TILE
    tile_expert = jnp.minimum(jnp.sum(ends[None, :] <= tile_start[:, None], axis=1),
                              N_EXPERTS - 1).astype(jnp.int32)
    n_used = (ends[-1] // MOE_TILE).astype(jnp.int32).reshape(1)
    src_tok = _slot_sources(pos, n_pad)

    def idx_specs(width, n_blocks):
        first = lambda *a: (0, 0, 0)
        nxt = lambda *a: (jnp.minimum(a[0] + 1, n_blocks - 1), 0, 0)
        return [pl.BlockSpec((1, 1, width), first, memory_space=pltpu.SMEM),
                pl.BlockSpec((1, 1, width), nxt, memory_space=pltpu.SMEM)]

    src3 = src_tok.reshape(n_tiles, 1, MOE_TILE)
    wspec = lambda shape: pl.BlockSpec(shape, lambda j, te, nu: (te[j], 0, 0))
    ys = pl.pallas_call(
        _moe_expert_body,
        out_shape=jax.ShapeDtypeStruct((n_pad, d), F32),
        grid_spec=pltpu.PrefetchScalarGridSpec(
            num_scalar_prefetch=2,
            grid=(n_tiles,),
            in_specs=idx_specs(MOE_TILE, n_tiles) + [
                pl.BlockSpec(memory_space=pl.ANY),
                wspec((1, d, D_FF)), wspec((1, d, D_FF)), wspec((1, D_FF, d))],
            out_specs=pl.BlockSpec((MOE_TILE, d), lambda j, te, nu: (j, 0)),
            scratch_shapes=[pltpu.VMEM((2, MOE_TILE, d), F32), pltpu.SemaphoreType.DMA((2,))]),
        compiler_params=_params("arbitrary"),
        name="moe_experts",
    )(tile_expert, n_used, src3, src3, hf, w_gate, w_up, w_down)

    tc = min(tm, MOE_TILE)
    pos3 = pos.reshape(t // tc, 1, 2 * tc)
    return pl.pallas_call(
        _moe_combine_body,
        out_shape=jax.ShapeDtypeStruct((t, d), F32),
        grid=(t // tc,),
        in_specs=idx_specs(2 * tc, t // tc) + [
            pl.BlockSpec((tc, d), lambda i: (i, 0)),
            pl.BlockSpec((tc, 2), lambda i: (i, 0)),
            pl.BlockSpec(memory_space=pl.ANY)],
        out_specs=pl.BlockSpec((tc, d), lambda i: (i, 0)),
        scratch_shapes=[pltpu.VMEM((2, 2, tc, d), F32), pltpu.SemaphoreType.DMA((2,))],
        compiler_params=_params("arbitrary"),
        name="moe_combine",
    )(pos3, pos3, x1, w_pair, ys)


def _split_hi_lo(w):
    hi = w.astype(BF16)
    return hi, (w - hi.astype(F32)).astype(BF16)


def kernel(x_prompt, x_sample, cache_k_diff, cache_v_diff, state_retention, attn_norm_g, w_in, da_q_norm_g, da_k_norm_g, da_lambda_q1, da_lambda_k1, da_lambda_q2, da_lambda_k2, da_out_norm_g, ret_out_norm_g, w_out, ffn_norm_g, w_group, b_group, w_expert, b_expert, w_gate, w_up, w_down):
    assert w_in.shape[0] == 1, "single-layer model"
    bp, seq, d = x_prompt.shape
    bd, t_dec, _ = x_sample.shape
    past = cache_k_diff.shape[2]
    assert bp == 1

    w_in_b = w_in[0].astype(BF16)
    w_out_b = w_out[0].astype(BF16)
    w_router = jnp.concatenate([w_expert[0].reshape(d, N_EXPERTS), w_group[0]], axis=1)
    w_router = jnp.pad(w_router, ((0, 0), (0, LANES - w_router.shape[1])))
    wr_hi, wr_lo = _split_hi_lo(w_router)
    b_router = jnp.concatenate([b_expert[0].reshape(N_EXPERTS), b_group[0]])
    b_router = jnp.pad(b_router, (0, LANES - b_router.shape[0])).reshape(1, LANES)

    lams = _lambda_args(da_lambda_q1[0], da_lambda_k1[0], da_lambda_q2[0], da_lambda_k2[0])

    def layer(x, cos, sin, attend, want_vt, ret_seq, ret_chunk, s0, tm):
        h = _rmsnorm(x, attn_norm_g[0], tm)
        q_da, k_da, k_da_b, v_da, v_da_b, v_da_t, q_r, k_r, v_r, g_r = _project(
            h, w_in_b, da_q_norm_g[0], da_k_norm_g[0], cos, sin, tm, want_vt)
        o_da = attend(q_da, k_da_b, v_da_b, v_da_t)
        o_ret, s_new = _retention(q_r, k_r, v_r, g_r, s0, ret_out_norm_g[0], ret_seq, ret_chunk)
        x1, hf, gate, sel = _outproj(x, o_da, o_ret, w_out_b, ffn_norm_g[0], wr_hi, wr_lo,
                                     b_router, min(tm, 256))
        y = _moe(hf, gate, sel, x1, w_gate[0], w_up[0], w_down[0], tm)
        return y, k_da, v_da, s_new

    cos_p, sin_p = _rope_tables(seq, 0)
    y_p, k_p, v_p, s_p = layer(
        x_prompt.reshape(seq, d), cos_p, sin_p,
        lambda q, k, v, vt: _da_prompt(q, k, vt[0], lams, da_out_norm_g[0], 512),
        True, seq, 256, jnp.zeros((bp, RET_HEADS, RET_DK, RET_DV), F32), 512)

    cos_s, sin_s = _rope_tables(t_dec, past)
    cos_s = jnp.tile(cos_s, (bd, 1))
    sin_s = jnp.tile(sin_s, (bd, 1))
    y_s, k_s, v_s, s_s = layer(
        x_sample.reshape(bd * t_dec, d), cos_s, sin_s,
        lambda q, k, v, vt: _da_sample(q, k, v, cache_k_diff[0], cache_v_diff[0], lams,
                                       da_out_norm_g[0], t_dec, 1024),
        False, t_dec, t_dec, state_retention[0], bd * t_dec)

    return (y_p.reshape(bp, seq, d),
            y_s.reshape(bd, t_dec, d),
            k_p.reshape(1, bp, seq, DA_HEADS, 2 * DA_DH),
            v_p.reshape(1, bp, seq, DA_HEADS, DA_DV),
            s_p.reshape(1, bp, RET_HEADS, RET_DK, RET_DV),
            k_s.reshape(1, bd, t_dec, DA_HEADS, 2 * DA_DH),
            v_s.reshape(1, bd, t_dec, DA_HEADS, DA_DV),
            s_s.reshape(1, bd, RET_HEADS, RET_DK, RET_DV))
```
